```python
import math
import jax, jax.numpy as jnp
from jax import lax
import numpy as np

D_MODEL = 2048
BATCH = 32
SEQ = 256
DEPTH = 4
DEC_BATCH = 8
DEC_SEQ = 4096
PAST_LEN = 256

GRID_W = 64
N_MIXERS = 3
N_POOL_LAYERS = (DEPTH + N_MIXERS - 1) // N_MIXERS
N_SGU_LAYERS = (DEPTH + N_MIXERS - 2) // N_MIXERS
N_ATTN_LAYERS = DEPTH // N_MIXERS
POOL_WINDOWS = (2, 4, 8, 16)
POOL_GROUPS = len(POOL_WINDOWS)
POOL_GROUP_DIM = D_MODEL // POOL_GROUPS
SGU_DIM = D_MODEL
SGU_GROUPS = 8
SGU_GROUP_DIM = SGU_DIM // SGU_GROUPS
SGU_CHUNK = 128
DA_HEAD_DIM = 128
DA_V_DIM = 2 * DA_HEAD_DIM
DA_HEADS = D_MODEL // DA_V_DIM
DA_QK_WIDTH = DA_HEADS * 2 * DA_HEAD_DIM
Q_BLOCK = 128
ROPE_BASE = 10000.0
ROPE_HALF = DA_HEAD_DIM // 4
FFN_HIDDEN = -(-8 * D_MODEL // (3 * 256)) * 256
N_MOD = 6
EPS = 1e-6

kernel_name = "hybrid_pool_sgu_diffattn_dit_step"


def rms_norm(x, g):
    xf = x.astype(jnp.float32)
    y = xf * lax.rsqrt(jnp.mean(xf * xf, axis=-1, keepdims=True) + EPS)
    return (y * g.astype(jnp.float32)).astype(x.dtype)


def ada_mod(cond, w, b):
    m = jax.nn.silu(cond) @ w + b
    return jnp.split(m[:, None, :], N_MOD, axis=-1)


def modulate(h, shift, scale):
    return h * (1 + scale) + shift


def multiscale_pool(h, w, scale):
    B, n, _ = h.shape
    hg = h.reshape(B, n, POOL_GROUPS, POOL_GROUP_DIM)
    cs = jnp.pad(jnp.cumsum(hg.astype(jnp.float32), axis=1), ((0, 0), (1, 0), (0, 0), (0, 0)))
    t = jnp.arange(n)
    pooled = []
    for g, win in enumerate(POOL_WINDOWS):
        lo = jnp.clip(t - win // 2, 0, n)
        hi = jnp.clip(t - win // 2 + win, 0, n)
        cnt = (hi - lo).astype(jnp.float32)
        pooled.append((cs[:, hi, g] - cs[:, lo, g]) / cnt[:, None])
    d = jnp.stack(pooled, axis=2) - hg.astype(jnp.float32)
    y = jnp.einsum('bngc,gcd->bngd', d.astype(h.dtype), w)
    return y.reshape(B, n, D_MODEL) * scale


def spatial_gating_mlp(h, w_in, norm_g, ws, bs, w_out):
    B, n, _ = h.shape
    z = jax.nn.gelu(h @ w_in, approximate=False)
    u, v = jnp.split(z, 2, axis=-1)
    v = rms_norm(v, norm_g).reshape(B, n // SGU_CHUNK, SGU_CHUNK, SGU_GROUPS, SGU_GROUP_DIM)
    v = jnp.einsum('gpq,bcqge->bcpge', ws, v) + bs.T[:, :, None]
    return (u * v.reshape(B, n, SGU_DIM)) @ w_out


def axial_rope_tables(n_tokens):
    rows = n_tokens // GRID_W
    row = jnp.broadcast_to(jnp.arange(rows, dtype=jnp.float32)[:, None], (rows, GRID_W)).reshape(-1)
    col = jnp.broadcast_to(jnp.arange(GRID_W, dtype=jnp.float32)[None, :], (rows, GRID_W)).reshape(-1)
    inv = ROPE_BASE ** (-jnp.arange(ROPE_HALF, dtype=jnp.float32) / ROPE_HALF)
    ang = jnp.stack([row[:, None] * inv, col[:, None] * inv], axis=1)
    return jnp.cos(ang), jnp.sin(ang)


def apply_axial_rope(x, cos, sin):
    xs = x.reshape(x.shape[:-1] + (2, 2, ROPE_HALF)).astype(jnp.float32)
    x1, x2 = xs[..., 0, :], xs[..., 1, :]
    c = cos[:, None, None]
    s = sin[:, None, None]
    out = jnp.stack([x1 * c - x2 * s, x2 * c + x1 * s], axis=-2)
    return out.reshape(x.shape).astype(x.dtype)


def diff_qkv(h, w):
    B, n, _ = h.shape
    q, k, v = jnp.split(h @ w, [DA_QK_WIDTH, 2 * DA_QK_WIDTH], axis=-1)
    return (q.reshape(B, n, DA_HEADS, 2, DA_HEAD_DIM),
            k.reshape(B, n, DA_HEADS, 2, DA_HEAD_DIM),
            v.reshape(B, n, DA_HEADS, DA_V_DIM))


def diff_lambda(lp, lam_init):
    lp = lp.astype(jnp.float32)
    return jnp.exp(jnp.sum(lp[0] * lp[1])) - jnp.exp(jnp.sum(lp[2] * lp[3])) + lam_init


def diff_attention(q, k, v, lam):
    B, n = q.shape[:2]
    nb = n // Q_BLOCK
    qb = q.reshape((B, nb, Q_BLOCK) + q.shape[2:]).swapaxes(0, 1)
    sc = DA_HEAD_DIM ** -0.5

    def block(qi):
        s = jnp.einsum('bqhmd,bkhmd->bhmqk', qi, k, preferred_element_type=jnp.float32) * sc
        p = jax.nn.softmax(s, axis=-1)
        p = p[:, :, 0] - lam * p[:, :, 1]
        return jnp.einsum('bhqk,bkhe->bqhe', p.astype(v.dtype), v)

    o = lax.map(block, qb)
    return o.swapaxes(0, 1).reshape(B, n, DA_HEADS, DA_V_DIM)


def diff_attn_out(o, subln_g, w_o, lam_init):
    B, n = o.shape[:2]
    o = rms_norm(o, subln_g) * (1.0 - lam_init)
    return o.reshape(B, n, DA_HEADS * DA_V_DIM) @ w_o


def swiglu(h, w_in, w_out):
    a, b = jnp.split(h @ w_in, 2, axis=-1)
    return (jax.nn.silu(a) * b) @ w_out


def setup_inputs(seed: int = 0) -> dict:
    key = jax.random.key(seed)
    ks = jax.random.split(key, 26)
    D = D_MODEL

    def nrm(k, shape, s):
        return jax.random.normal(k, shape, jnp.float32) * s

    return {
        "x_prompt": nrm(ks[0], (BATCH, SEQ, D), 1.0),
        "x_sample": nrm(ks[1], (DEC_BATCH, DEC_SEQ, D), 1.0),
        "cache_k": nrm(ks[2], (DEC_BATCH, N_ATTN_LAYERS, PAST_LEN, DA_HEADS, 2, DA_HEAD_DIM), 1.0),
        "cache_v": nrm(ks[3], (DEC_BATCH, N_ATTN_LAYERS, PAST_LEN, DA_HEADS, DA_V_DIM), 1.0),
        "c": nrm(ks[4], (DEC_BATCH, D), 1.0),
        "c_ctx": nrm(ks[5], (D,), 1.0),
        "ada_w": nrm(ks[6], (DEPTH, D, N_MOD * D), 0.5 * D ** -0.5),
        "ada_b": nrm(ks[7], (DEPTH, N_MOD * D), 0.02),
        "norm_mix_g": 1.0 + nrm(ks[8], (DEPTH, D), 0.05),
        "norm_ffn_g": 1.0 + nrm(ks[9], (DEPTH, D), 0.05),
        "pool_w": nrm(ks[10], (N_POOL_LAYERS, POOL_GROUPS, POOL_GROUP_DIM, POOL_GROUP_DIM), POOL_GROUP_DIM ** -0.5),
        "pool_scale": 1.0 + nrm(ks[11], (N_POOL_LAYERS, D), 0.1),
        "sgu_w_in": nrm(ks[12], (N_SGU_LAYERS, D, 2 * SGU_DIM), D ** -0.5),
        "sgu_norm_g": 1.0 + nrm(ks[13], (N_SGU_LAYERS, SGU_DIM), 0.05),
        "sgu_ws": nrm(ks[14], (N_SGU_LAYERS, SGU_GROUPS, SGU_CHUNK, SGU_CHUNK), SGU_CHUNK ** -0.5),
        "sgu_b": 1.0 + nrm(ks[15], (N_SGU_LAYERS, SGU_GROUPS, SGU_CHUNK), 0.1),
        "sgu_w_out": nrm(ks[16], (N_SGU_LAYERS, SGU_DIM, D), SGU_DIM ** -0.5),
        "attn_w_qkv": nrm(ks[17], (N_ATTN_LAYERS, D, 2 * DA_QK_WIDTH + DA_HEADS * DA_V_DIM), D ** -0.5),
        "attn_lambda": nrm(ks[18], (N_ATTN_LAYERS, 4, DA_HEAD_DIM), 0.1),
        "attn_subln_g": 1.0 + nrm(ks[19], (N_ATTN_LAYERS, DA_V_DIM), 0.05),
        "attn_w_o": nrm(ks[20], (N_ATTN_LAYERS, DA_HEADS * DA_V_DIM, D), (DA_HEADS * DA_V_DIM) ** -0.5),
        "ffn_w_in": nrm(ks[21], (DEPTH, D, 2 * FFN_HIDDEN), D ** -0.5),
        "ffn_w_out": nrm(ks[22], (DEPTH, FFN_HIDDEN, D), FFN_HIDDEN ** -0.5),
        "final_g": 1.0 + nrm(ks[23], (D,), 0.05),
    }


def reference(x_prompt, x_sample, cache_k, cache_v, c, c_ctx, ada_w, ada_b, norm_mix_g, norm_ffn_g,
              pool_w, pool_scale, sgu_w_in, sgu_norm_g, sgu_ws, sgu_b, sgu_w_out,
              attn_w_qkv, attn_lambda, attn_subln_g, attn_w_o, ffn_w_in, ffn_w_out, final_g):
    xc = x_prompt
    xl = x_sample
    cos, sin = axial_rope_tables(xl.shape[1])
    new_k, new_v = [], []
    for i in range(DEPTH):
        kind = i % N_MIXERS
        slot = i // N_MIXERS
        mc = ada_mod(c_ctx[None, :], ada_w[i], ada_b[i])
        ml = ada_mod(c, ada_w[i], ada_b[i])
        hc = modulate(rms_norm(xc, norm_mix_g[i]), mc[0], mc[1])
        hl = modulate(rms_norm(xl, norm_mix_g[i]), ml[0], ml[1])
        if kind == 0:
            oc = multiscale_pool(hc, pool_w[slot], pool_scale[slot])
            ol = multiscale_pool(hl, pool_w[slot], pool_scale[slot])
        elif kind == 1:
            sgu_args = (sgu_w_in[slot], sgu_norm_g[slot], sgu_ws[slot], sgu_b[slot], sgu_w_out[slot])
            oc = spatial_gating_mlp(hc, *sgu_args)
            ol = spatial_gating_mlp(hl, *sgu_args)
        else:
            lam_init = 0.8 - 0.6 * math.exp(-0.3 * i)
            lam = diff_lambda(attn_lambda[slot], lam_init)
            qc, kc, vc = diff_qkv(hc, attn_w_qkv[slot])
            ql, kl, vl = diff_qkv(hl, attn_w_qkv[slot])
            ql = apply_axial_rope(ql, cos, sin)
            kl = apply_axial_rope(kl, cos, sin)
            new_k.append(kc)
            new_v.append(vc)
            oc = diff_attention(qc, kc, vc, lam)
            k_all = jnp.concatenate([cache_k[:, slot].astype(kl.dtype), kl], axis=1)
            v_all = jnp.concatenate([cache_v[:, slot].astype(vl.dtype), vl], axis=1)
            ol = diff_attention(ql, k_all, v_all, lam)
            oc = diff_attn_out(oc, attn_subln_g[slot], attn_w_o[slot], lam_init)
            ol = diff_attn_out(ol, attn_subln_g[slot], attn_w_o[slot], lam_init)
        xc = xc + mc[2] * oc
        xl = xl + ml[2] * ol
        hc = modulate(rms_norm(xc, norm_ffn_g[i]), mc[3], mc[4])
        hl = modulate(rms_norm(xl, norm_ffn_g[i]), ml[3], ml[4])
        xc = xc + mc[5] * swiglu(hc, ffn_w_in[i], ffn_w_out[i])
        xl = xl + ml[5] * swiglu(hl, ffn_w_in[i], ffn_w_out[i])
    y_prompt = rms_norm(xc, final_g)
    y_sample = rms_norm(xl, final_g)
    new_cache_k = jnp.stack(new_k, axis=1)
    new_cache_v = jnp.stack(new_v, axis=1)
    return (y_prompt, y_sample, new_cache_k, new_cache_v)
```

```python
import functools
import math

import jax
import jax.numpy as jnp
from jax import lax
from jax.experimental import pallas as pl
from jax.experimental.pallas import tpu as pltpu

EPS = 1e-6
N_MOD = 6
N_MIXERS = 3
POOL_WINDOWS = (2, 4, 8, 16)
POOL_HALO = 8
SGU_GROUPS = 8
SGU_CHUNK = 128
HEAD_DIM = 128
V_DIM = 2 * HEAD_DIM
GRID_W = 64
ROPE_BASE = 10000.0
ROPE_HALF = HEAD_DIM // 4
LOG2E = 1.4426950408889634

MXU_DTYPE = jnp.bfloat16
COND_ROWS = 16
V7X_VMEM_BYTES = 64 * 1024 * 1024
VMEM_CAP_BYTES = V7X_VMEM_BYTES - 8 * 1024 * 1024


def _vmem_limit(estimate_bytes):
    return int(min(max(estimate_bytes * 5 // 4, 16 * 1024 * 1024), VMEM_CAP_BYTES))


def _params(semantics, vmem_estimate):
    return pltpu.CompilerParams(dimension_semantics=semantics,
                                vmem_limit_bytes=_vmem_limit(vmem_estimate))


def _dot(a, b):
    return jnp.dot(a, b, preferred_element_type=jnp.float32)


def _norm_mod(x, g, shift, scale):
    ms = jnp.mean(x * x, axis=-1, keepdims=True)
    return (x * lax.rsqrt(ms + EPS) * g) * (1.0 + scale) + shift


def _largest_tile(n, cap, quantum):
    t = min(cap, n)
    t -= t % quantum
    while n % t:
        t -= quantum
    return t


def _ada_kernel(cond_ref, w_ref, b_ref, o_ref):
    c = cond_ref[...]
    s = (c * jax.nn.sigmoid(c)).astype(MXU_DTYPE)
    o_ref[...] = _dot(s, w_ref[...].astype(MXU_DTYPE)) + b_ref[...]


def _ada_mods(cond, ada_w, ada_b):
    depth, d, n = ada_w.shape
    bn = _largest_tile(n, 1024, 128)
    est = 2 * d * bn * 4 + d * bn * 2 + 4 * COND_ROWS * (d + bn) * 4
    return pl.pallas_call(
        _ada_kernel,
        grid=(depth, n // bn),
        in_specs=[
            pl.BlockSpec((COND_ROWS, d), lambda l, j: (0, 0)),
            pl.BlockSpec((None, d, bn), lambda l, j: (l, 0, j)),
            pl.BlockSpec((None, 1, bn), lambda l, j: (l, 0, j)),
        ],
        out_specs=pl.BlockSpec((None, COND_ROWS, bn), lambda l, j: (l, 0, j)),
        out_shape=jax.ShapeDtypeStruct((depth, COND_ROWS, n), jnp.float32),
        compiler_params=_params(("parallel", "parallel"), est),
        name="ada_mods",
    )(cond, ada_w, ada_b.reshape(depth, 1, n))


def _ffn_kernel(x_ref, mod_ref, g_ref, wa_ref, wb_ref, wo_ref, *rest, n_hidden_blocks, final_norm):
    if final_norm:
        fg_ref, o_ref, h_sc, acc_sc = rest
    else:
        o_ref, h_sc, acc_sc = rest
    j = pl.program_id(1)

    @pl.when(j == 0)
    def _():
        h = _norm_mod(x_ref[...], g_ref[...], mod_ref[3:4, :], mod_ref[4:5, :])
        h_sc[...] = h.astype(MXU_DTYPE)
        acc_sc[...] = jnp.zeros_like(acc_sc)

    h = h_sc[...]
    a = _dot(h, wa_ref[...])
    b = _dot(h, wb_ref[...])
    act = (a * jax.nn.sigmoid(a)) * b
    acc_sc[...] += _dot(act.astype(MXU_DTYPE), wo_ref[...])

    @pl.when(j == n_hidden_blocks - 1)
    def _():
        y = x_ref[...] + mod_ref[5:6, :] * acc_sc[...]
        if final_norm:
            ms = jnp.mean(y * y, axis=-1, keepdims=True)
            y = y * lax.rsqrt(ms + EPS) * fg_ref[...]
        o_ref[...] = y


def _ffn(x, mods, mod_row, g, w_in, w_out, final_g):
    t, d = x.shape
    f = w_out.shape[0]
    tm = _largest_tile(t, 512, 8)
    th = _largest_tile(f, 512, 128)
    nj = f // th
    final_norm = final_g is not None
    in_specs = [
        pl.BlockSpec((tm, d), lambda i, j: (i, 0)),
        pl.BlockSpec((None, N_MOD, d), lambda i, j: (mod_row(i, tm), 0, 0)),
        pl.BlockSpec((1, d), lambda i, j: (0, 0)),
        pl.BlockSpec((d, th), lambda i, j: (0, j)),
        pl.BlockSpec((d, th), lambda i, j: (0, j + nj)),
        pl.BlockSpec((th, d), lambda i, j: (j, 0)),
    ]
    args = [x, mods, g.reshape(1, d), w_in, w_in, w_out]
    if final_norm:
        in_specs.append(pl.BlockSpec((1, d), lambda i, j: (0, 0)))
        args.append(final_g.reshape(1, d))
    est = (4 * tm * d * 4 + tm * d * 4 + tm * d * 2 + 6 * d * th * 2 + 4 * tm * th * 4)
    return pl.pallas_call(
        functools.partial(_ffn_kernel, n_hidden_blocks=nj, final_norm=final_norm),
        grid=(t // tm, nj),
        in_specs=in_specs,
        out_specs=pl.BlockSpec((tm, d), lambda i, j: (i, 0)),
        out_shape=jax.ShapeDtypeStruct((t, d), jnp.float32),
        scratch_shapes=[pltpu.VMEM((tm, d), MXU_DTYPE), pltpu.VMEM((tm, d), jnp.float32)],
        compiler_params=_params(("parallel", "arbitrary"), est),
        name="ffn",
    )(*args)


def _pool_kernel(x_ref, xp_ref, xn_ref, mod_ref, g_ref, w_ref, sc_ref, o_ref, h_sc, d_sc,
                 *, seq_len, blocks_per_seq):
    tm, d = x_ref.shape
    gd = d // len(POOL_WINDOWS)
    i = pl.program_id(0)
    blk = i % blocks_per_seq
    g = g_ref[...]
    shift = mod_ref[0:1, :]
    scale = mod_ref[1:2, :]
    x = x_ref[...]
    has_prev = (blk > 0).astype(jnp.float32)
    has_next = (blk < blocks_per_seq - 1).astype(jnp.float32)
    h_sc[0:POOL_HALO, :] = _norm_mod(xp_ref[...], g, shift, scale) * has_prev
    h_sc[POOL_HALO:POOL_HALO + tm, :] = _norm_mod(x, g, shift, scale)
    h_sc[POOL_HALO + tm:, :] = _norm_mod(xn_ref[...], g, shift, scale) * has_next

    pos = blk * tm + lax.broadcasted_iota(jnp.int32, (tm, 1), 0)
    for gi, win in enumerate(POOL_WINDOWS):
        cols = slice(gi * gd, (gi + 1) * gd)
        half = win // 2
        total = h_sc[POOL_HALO - half:POOL_HALO - half + tm, cols]
        for k in range(1 - half, half):
            total = total + h_sc[POOL_HALO + k:POOL_HALO + k + tm, cols]
        cnt = jnp.minimum(pos + half, seq_len) - jnp.maximum(pos - half, 0)
        dev = total / cnt.astype(jnp.float32) - h_sc[POOL_HALO:POOL_HALO + tm, cols]
        d_sc[:, cols] = _dot(dev.astype(MXU_DTYPE), w_ref[gi])
    o_ref[...] = x + mod_ref[2:3, :] * (d_sc[...] * sc_ref[...])


def _pool_mixer(x, mods, mod_row, g, w, scale, seq_len):
    t, d = x.shape
    tm = _largest_tile(seq_len, 512, POOL_HALO)
    bps = seq_len // tm
    hb = tm // POOL_HALO
    n_halo_blocks = t // POOL_HALO
    gd = d // len(POOL_WINDOWS)
    est = 4 * tm * d * 4 + 2 * (tm + 2 * POOL_HALO) * d * 4 + 2 * w.size * 2 + 4 * tm * gd * 4
    return pl.pallas_call(
        functools.partial(_pool_kernel, seq_len=seq_len, blocks_per_seq=bps),
        grid=(t // tm,),
        in_specs=[
            pl.BlockSpec((tm, d), lambda i: (i, 0)),
            pl.BlockSpec((POOL_HALO, d), lambda i: (jnp.maximum(i * hb - 1, 0), 0)),
            pl.BlockSpec((POOL_HALO, d), lambda i: (jnp.minimum((i + 1) * hb, n_halo_blocks - 1), 0)),
            pl.BlockSpec((None, N_MOD, d), lambda i: (mod_row(i, tm), 0, 0)),
            pl.BlockSpec((1, d), lambda i: (0, 0)),
            pl.BlockSpec(w.shape, lambda i: (0, 0, 0)),
            pl.BlockSpec((1, d), lambda i: (0, 0)),
        ],
        out_specs=pl.BlockSpec((tm, d), lambda i: (i, 0)),
        out_shape=jax.ShapeDtypeStruct((t, d), jnp.float32),
        scratch_shapes=[pltpu.VMEM((tm + 2 * POOL_HALO, d), jnp.float32),
                        pltpu.VMEM((tm, d), jnp.float32)],
        compiler_params=_params(("parallel",), est),
        name="pool_mixer",
    )(x, x, x, mods, g.reshape(1, d), w, scale.reshape(1, d))


def _sgu_kernel(x_ref, mod_ref, g_ref, win_ref, ng_ref, ws_ref, bs_ref, wout_ref, o_ref,
                h_sc, v_sc, p_sc):
    tm, d = x_ref.shape
    sd = wout_ref.shape[0]
    gd = sd // SGU_GROUPS
    x = x_ref[...]
    h_sc[...] = _norm_mod(x, g_ref[...], mod_ref[0:1, :], mod_ref[1:2, :]).astype(MXU_DTYPE)
    h = h_sc[...]

    def gelu(z):
        return 0.5 * z * (1.0 + lax.erf(z * (1.0 / math.sqrt(2.0))))

    ssq = jnp.zeros((tm, 1), jnp.float32)
    for gi in range(SGU_GROUPS):
        v = gelu(_dot(h, win_ref[:, sd + gi * gd:sd + (gi + 1) * gd]))
        ssq = ssq + jnp.sum(v * v, axis=-1, keepdims=True)
        v_sc[:, gi * gd:(gi + 1) * gd] = v
    rstd = lax.rsqrt(ssq * (1.0 / sd) + EPS)

    for gi in range(SGU_GROUPS):
        cols = slice(gi * gd, (gi + 1) * gd)
        vn = (v_sc[:, cols] * rstd * ng_ref[:, cols]).astype(MXU_DTYPE)
        u = gelu(_dot(h, win_ref[:, cols]))
        wsg = ws_ref[gi]
        bias = bs_ref[gi]
        for c in range(tm // SGU_CHUNK):
            rows = slice(c * SGU_CHUNK, (c + 1) * SGU_CHUNK)
            vm = _dot(wsg, vn[rows, :]) + bias
            p_sc[rows, cols] = (u[rows, :] * vm).astype(MXU_DTYPE)
    o_ref[...] = x + mod_ref[2:3, :] * _dot(p_sc[...], wout_ref[...])


def _sgu_mixer(x, mods, mod_row, g, w_in, norm_g, ws, bs, w_out):
    t, d = x.shape
    sd = w_out.shape[0]
    tm = _largest_tile(t, 256, SGU_CHUNK)
    est = (w_in.size + w_out.size + ws.size) * 2 + 4 * tm * d * 4 + tm * sd * 4 + 2 * tm * d * 2 \
        + 6 * tm * (sd // SGU_GROUPS) * 4
    const = dict(pipeline_mode=pl.Buffered(1))
    return pl.pallas_call(
        _sgu_kernel,
        grid=(t // tm,),
        in_specs=[
            pl.BlockSpec((tm, d), lambda i: (i, 0)),
            pl.BlockSpec((None, N_MOD, d), lambda i: (mod_row(i, tm), 0, 0)),
            pl.BlockSpec((1, d), lambda i: (0, 0)),
            pl.BlockSpec(w_in.shape, lambda i: (0, 0), **const),
            pl.BlockSpec((1, sd), lambda i: (0, 0)),
            pl.BlockSpec(ws.shape, lambda i: (0, 0, 0), **const),
            pl.BlockSpec(bs.shape, lambda i: (0, 0, 0), **const),
            pl.BlockSpec(w_out.shape, lambda i: (0, 0), **const),
        ],
        out_specs=pl.BlockSpec((tm, d), lambda i: (i, 0)),
        out_shape=jax.ShapeDtypeStruct((t, d), jnp.float32),
        scratch_shapes=[pltpu.VMEM((tm, d), MXU_DTYPE), pltpu.VMEM((tm, sd), jnp.float32),
                        pltpu.VMEM((tm, sd), MXU_DTYPE)],
        compiler_params=_params(("parallel",), est),
        name="sgu_mixer",
    )(x, mods, g.reshape(1, d), w_in, norm_g.reshape(1, sd), ws, bs, w_out)


def _rope_tables(n_tokens):
    rows = n_tokens // GRID_W
    row = jnp.broadcast_to(jnp.arange(rows, dtype=jnp.float32)[:, None], (rows, GRID_W)).reshape(-1)
    col = jnp.broadcast_to(jnp.arange(GRID_W, dtype=jnp.float32)[None, :], (rows, GRID_W)).reshape(-1)
    inv = ROPE_BASE ** (-jnp.arange(ROPE_HALF, dtype=jnp.float32) / ROPE_HALF)
    ang = jnp.stack([row[:, None] * inv, col[:, None] * inv], axis=1)
    cos, sin = jnp.cos(ang), jnp.sin(ang)
    c = jnp.stack([cos, cos], axis=2).reshape(n_tokens, HEAD_DIM)
    s = jnp.stack([-sin, sin], axis=2).reshape(n_tokens, HEAD_DIM)
    return c, s


def _qkv_kernel(x_ref, mod_ref, g_ref, w_ref, *rest, rope, keep_f32, q_scale):
    rest = list(rest)
    if rope:
        c_ref, s_ref = rest[:2]
        rest = rest[2:]
    q_ref, k_ref, v_ref = rest[:3]
    rest = rest[3:]
    if keep_f32:
        kf_ref, vf_ref = rest[:2]
        rest = rest[2:]
    (h_sc,) = rest
    tm, d = x_ref.shape
    j = pl.program_id(1)

    @pl.when(j == 0)
    def _():
        h = _norm_mod(x_ref[...], g_ref[...], mod_ref[0:1, :], mod_ref[1:2, :])
        h_sc[...] = h.astype(MXU_DTYPE)

    def rotate(dst_ref, y, mult):
        c = c_ref[...]
        s = s_ref[...]
        lane = lax.broadcasted_iota(jnp.int32, (tm, HEAD_DIM), 1)
        low_half = (lane % (2 * ROPE_HALF)) < ROPE_HALF
        for gi in range(d // HEAD_DIM):
            cols = slice(gi * HEAD_DIM, (gi + 1) * HEAD_DIM)
            yg = y[:, cols]
            partner = jnp.where(low_half,
                                pltpu.roll(yg, HEAD_DIM - ROPE_HALF, 1),
                                pltpu.roll(yg, ROPE_HALF, 1))
            r = yg * c + partner * s
            if mult != 1.0:
                r = r * mult
            dst_ref[:, cols] = r.astype(dst_ref.dtype)

    @pl.when(j == 0)
    def _():
        y = _dot(h_sc[...], w_ref[...])
        if rope:
            rotate(q_ref, y, q_scale)
        else:
            q_ref[...] = (y * q_scale).astype(q_ref.dtype)

    @pl.when(j == 1)
    def _():
        y = _dot(h_sc[...], w_ref[...])
        if keep_f32:
            kf_ref[...] = y
        if rope:
            rotate(k_ref, y, 1.0)
        else:
            k_ref[...] = y.astype(k_ref.dtype)

    @pl.when(j == 2)
    def _():
        y = _dot(h_sc[...], w_ref[...])
        if keep_f32:
            vf_ref[...] = y
        v_ref[...] = y.astype(v_ref.dtype)


def _qkv(x, mods, mod_row, g, w_qkv, seq_len, rope, keep_f32):
    t, d = x.shape
    tm = _largest_tile(seq_len, 512, 8)
    bps = seq_len // tm
    q_scale = HEAD_DIM ** -0.5 * LOG2E
    in_specs = [
        pl.BlockSpec((tm, d), lambda i, j: (i, 0)),
        pl.BlockSpec((None, N_MOD, d), lambda i, j: (mod_row(i, tm), 0, 0)),
        pl.BlockSpec((1, d), lambda i, j: (0, 0)),
        pl.BlockSpec((d, d), lambda i, j: (0, j)),
    ]
    args = [x, mods, g.reshape(1, d), w_qkv]
    if rope:
        c, s = _rope_tables(seq_len)
        in_specs += [pl.BlockSpec((tm, HEAD_DIM), lambda i, j: (i % bps, 0))] * 2
        args += [c, s]
    row_spec = pl.BlockSpec((tm, d), lambda i, j: (i, 0))
    out_specs = [row_spec] * 3
    out_shape = [jax.ShapeDtypeStruct((t, d), MXU_DTYPE)] * 3
    if keep_f32:
        out_specs += [row_spec] * 2
        out_shape += [jax.ShapeDtypeStruct((t, d), jnp.float32)] * 2
    est = 2 * tm * d * 4 + 2 * d * d * 2 + tm * d * 2 + 6 * tm * d * 2 + 3 * tm * d * 4 \
        + (4 * tm * d * 4 if keep_f32 else 0)
    return pl.pallas_call(
        functools.partial(_qkv_kernel, rope=rope, keep_f32=keep_f32, q_scale=q_scale),
        grid=(t // tm, 3),
        in_specs=in_specs,
        out_specs=out_specs,
        out_shape=out_shape,
        scratch_shapes=[pltpu.VMEM((tm, d), MXU_DTYPE)],
        compiler_params=_params(("parallel", "arbitrary"), est),
        name="attn_qkv",
    )(*args)


def _attn_kernel(q_ref, k_ref, v_ref, lp_ref, sg_ref, o_ref, *, lam_init):
    lp = lp_ref[...]
    lam = (jnp.exp(jnp.sum(lp[0:1, :] * lp[1:2, :], axis=-1, keepdims=True))
           - jnp.exp(jnp.sum(lp[2:3, :] * lp[3:4, :], axis=-1, keepdims=True)) + lam_init)
    v = v_ref[...]
    outs = []
    for mp in range(2):
        cols = slice(mp * HEAD_DIM, (mp + 1) * HEAD_DIM)
        s = lax.dot_general(q_ref[:, cols], k_ref[:, cols], (((1,), (1,)), ((), ())),
                            preferred_element_type=jnp.float32)
        p = jnp.exp2(s - jnp.max(s, axis=-1, keepdims=True))
        denom = jnp.sum(p, axis=-1, keepdims=True)
        outs.append(_dot(p.astype(MXU_DTYPE), v) / denom)
    o = outs[0] - lam * outs[1]
    ms = jnp.mean(o * o, axis=-1, keepdims=True)
    o = (o * lax.rsqrt(ms + EPS) * sg_ref[...]) * (1.0 - lam_init)
    o_ref[...] = o.astype(o_ref.dtype)


def _attention(q, k, v, lam_params, subln_g, lam_init):
    b, n, d = q.shape
    m = k.shape[1]
    heads = d // V_DIM
    tq = _largest_tile(n, 256, 8)
    est = 4 * tq * V_DIM * 2 + 4 * m * V_DIM * 2 + 2 * tq * m * 4 + 2 * tq * m * 2 + 4 * tq * V_DIM * 4
    return pl.pallas_call(
        functools.partial(_attn_kernel, lam_init=lam_init),
        grid=(b, heads, n // tq),
        in_specs=[
            pl.BlockSpec((None, tq, V_DIM), lambda bi, hi, qi: (bi, qi, hi)),
            pl.BlockSpec((None, m, V_DIM), lambda bi, hi, qi: (bi, 0, hi)),
            pl.BlockSpec((None, m, V_DIM), lambda bi, hi, qi: (bi, 0, hi)),
            pl.BlockSpec(lam_params.shape, lambda bi, hi, qi: (0, 0)),
            pl.BlockSpec((1, V_DIM), lambda bi, hi, qi: (0, 0)),
        ],
        out_specs=pl.BlockSpec((None, tq, V_DIM), lambda bi, hi, qi: (bi, qi, hi)),
        out_shape=jax.ShapeDtypeStruct((b, n, d), MXU_DTYPE),
        compiler_params=_params(("parallel", "parallel", "parallel"), est),
        name="diff_attention",
    )(q, k, v, lam_params, subln_g.reshape(1, V_DIM))


def _proj_kernel(a_ref, w_ref, x_ref, mod_ref, o_ref):
    o_ref[...] = x_ref[...] + mod_ref[2:3, :] * _dot(a_ref[...], w_ref[...])


def _attn_out_proj(a, w_o, x, mods, mod_row):
    t, d = x.shape
    tm = _largest_tile(t, 512, 8)
    est = 2 * tm * d * 2 + 2 * d * d * 2 + 5 * tm * d * 4
    return pl.pallas_call(
        _proj_kernel,
        grid=(t // tm,),
        in_specs=[
            pl.BlockSpec((tm, d), lambda i: (i, 0)),
            pl.BlockSpec((d, d), lambda i: (0, 0)),
            pl.BlockSpec((tm, d), lambda i: (i, 0)),
            pl.BlockSpec((None, N_MOD, d), lambda i: (mod_row(i, tm), 0, 0)),
        ],
        out_specs=pl.BlockSpec((tm, d), lambda i: (i, 0)),
        out_shape=jax.ShapeDtypeStruct((t, d), jnp.float32),
        compiler_params=_params(("parallel",), est),
        name="attn_out_proj",
    )(a, w_o, x, mods)


def kernel(x_prompt, x_sample, cache_k, cache_v, c, c_ctx, ada_w, ada_b, norm_mix_g, norm_ffn_g,
           pool_w, pool_scale, sgu_w_in, sgu_norm_g, sgu_ws, sgu_b, sgu_w_out,
           attn_w_qkv, attn_lambda, attn_subln_g, attn_w_o, ffn_w_in, ffn_w_out, final_g):
    batch, seq, d = x_prompt.shape
    dec_batch, dec_seq, _ = x_sample.shape
    depth = ada_w.shape[0]
    assert dec_batch < COND_ROWS and d % V_DIM == 0 and dec_seq % GRID_W == 0

    cond = jnp.zeros((COND_ROWS, d), jnp.float32).at[:dec_batch].set(c).at[dec_batch].set(c_ctx)
    mods = _ada_mods(cond, ada_w, ada_b).reshape(depth, COND_ROWS, N_MOD, d)

    cast = lambda w: w.astype(MXU_DTYPE)
    streams = {
        "ctx": (x_prompt.reshape(batch * seq, d), seq, lambda i, tm: dec_batch),
        "lat": (x_sample.reshape(dec_batch * dec_seq, d), dec_seq,
                lambda i, tm: (i * tm) // dec_seq),
    }
    xs = {name: s[0] for name, s in streams.items()}
    new_k, new_v = [], []
    for i in range(depth):
        kind, slot = i % N_MIXERS, i // N_MIXERS
        m_i = mods[i]
        for name, (_, n, mod_row) in streams.items():
            x = xs[name]
            if kind == 0:
                x = _pool_mixer(x, m_i, mod_row, norm_mix_g[i], cast(pool_w[slot]), pool_scale[slot], n)
            elif kind == 1:
                x = _sgu_mixer(x, m_i, mod_row, norm_mix_g[i], cast(sgu_w_in[slot]), sgu_norm_g[slot],
                               cast(sgu_ws[slot]), sgu_b[slot][:, :, None], cast(sgu_w_out[slot]))
            else:
                lam_init = 0.8 - 0.6 * math.exp(-0.3 * i)
                is_ctx = name == "ctx"
                outs = _qkv(x, m_i, mod_row, norm_mix_g[i], cast(attn_w_qkv[slot]), n,
                            rope=not is_ctx, keep_f32=is_ctx)
                nb = x.shape[0] // n
                q, k, v = (a.reshape(nb, n, d) for a in outs[:3])
                if is_ctx:
                    new_k.append(outs[3])
                    new_v.append(outs[4])
                else:
                    k = jnp.concatenate([cast(cache_k[:, slot]).reshape(nb, -1, d), k], axis=1)
                    v = jnp.concatenate([cast(cache_v[:, slot]).reshape(nb, -1, d), v], axis=1)
                a = _attention(q, k, v, attn_lambda[slot], attn_subln_g[slot], lam_init)
                x = _attn_out_proj(a.reshape(-1, d), cast(attn_w_o[slot]), x, m_i, mod_row)
            xs[name] = _ffn(x, m_i, mod_row, norm_ffn_g[i], cast(ffn_w_in[i]), cast(ffn_w_out[i]),
                            final_g if i == depth - 1 else None)
    heads = d // V_DIM
    y_prompt = xs["ctx"].reshape(batch, seq, d)
    y_sample = xs["lat"].reshape(dec_batch, dec_seq, d)
    new_cache_k = jnp.stack([a.reshape(batch, seq, heads, 2, HEAD_DIM) for a in new_k], axis=1)
    new_cache_v = jnp.stack([a.reshape(batch, seq, heads, V_DIM) for a in new_v], axis=1)
    return (y_prompt, y_sample, new_cache_k, new_cache_v)
```

```python
import functools
import math

import jax
import jax.numpy as jnp
from jax import lax
from jax.experimental import pallas as pl
from jax.experimental.pallas import tpu as pltpu

EPS = 1e-6
N_MOD = 6
N_MIXERS = 3
POOL_WINDOWS = (2, 4, 8, 16)
POOL_HALO = 8
SGU_GROUPS = 8
SGU_CHUNK = 128
HEAD_DIM = 128
V_DIM = 2 * HEAD_DIM
GRID_W = 64
ROPE_BASE = 10000.0
ROPE_HALF = HEAD_DIM // 4
LOG2E = 1.4426950408889634
KV_CHUNK = 512

MXU_DTYPE = jnp.bfloat16
COND_ROWS = 16
V7X_VMEM_BYTES = 64 * 1024 * 1024
VMEM_CAP_BYTES = V7X_VMEM_BYTES - 8 * 1024 * 1024


def _vmem_limit(estimate_bytes):
    return int(min(max(estimate_bytes * 5 // 4, 16 * 1024 * 1024), VMEM_CAP_BYTES))


def _params(semantics, vmem_estimate):
    return pltpu.CompilerParams(dimension_semantics=semantics,
                                vmem_limit_bytes=_vmem_limit(vmem_estimate))


def _dot(a, b):
    return jnp.dot(a, b, preferred_element_type=jnp.float32)


def _norm_mod(x, g, shift, scale):
    ms = jnp.mean(x * x, axis=-1, keepdims=True)
    return (x * lax.rsqrt(ms + EPS)) * (g * (1.0 + scale)) + shift


def _emit_ffn_input(h2_ref, x_new, gf_ref, mod_ref):
    h2_ref[...] = _norm_mod(x_new, gf_ref[...], mod_ref[3:4, :], mod_ref[4:5, :]).astype(h2_ref.dtype)


def _largest_tile(n, cap, quantum):
    t = min(cap, n)
    t -= t % quantum
    while n % t:
        t -= quantum
    return t


def _ada_kernel(cond_ref, w_ref, b_ref, o_ref):
    c = cond_ref[...]
    s = (c * jax.nn.sigmoid(c)).astype(MXU_DTYPE)
    o_ref[...] = _dot(s, w_ref[...].astype(MXU_DTYPE)) + b_ref[...]


def _ada_mods(cond, ada_w, ada_b):
    depth, d, n = ada_w.shape
    bn = _largest_tile(n, 1024, 128)
    est = 2 * d * bn * 4 + d * bn * 2 + 4 * COND_ROWS * (d + bn) * 4
    return pl.pallas_call(
        _ada_kernel,
        grid=(depth, n // bn),
        in_specs=[
            pl.BlockSpec((COND_ROWS, d), lambda l, j: (0, 0)),
            pl.BlockSpec((None, d, bn), lambda l, j: (l, 0, j)),
            pl.BlockSpec((None, 1, bn), lambda l, j: (l, 0, j)),
        ],
        out_specs=pl.BlockSpec((None, COND_ROWS, bn), lambda l, j: (l, 0, j)),
        out_shape=jax.ShapeDtypeStruct((depth, COND_ROWS, n), jnp.float32),
        compiler_params=_params(("parallel", "parallel"), est),
        name="ada_mods",
    )(cond, ada_w, ada_b.reshape(depth, 1, n))


def _ffn_kernel(x_ref, h_ref, mod_ref, wa_ref, wb_ref, wo_ref, *rest, n_hidden_blocks, final_norm):
    if final_norm:
        fg_ref, o_ref = rest
    else:
        (o_ref,) = rest
    j = pl.program_id(1)

    @pl.when(j == 0)
    def _():
        o_ref[...] = jnp.zeros_like(o_ref)

    h = h_ref[...]
    a = _dot(h, wa_ref[...])
    b = _dot(h, wb_ref[...])
    act = (a * jax.nn.sigmoid(a)) * b
    o_ref[...] += _dot(act.astype(MXU_DTYPE), wo_ref[...])

    @pl.when(j == n_hidden_blocks - 1)
    def _():
        y = x_ref[...] + mod_ref[5:6, :] * o_ref[...]
        if final_norm:
            ms = jnp.mean(y * y, axis=-1, keepdims=True)
            y = y * lax.rsqrt(ms + EPS) * fg_ref[...]
        o_ref[...] = y


def _ffn(x, h, mods, mod_row, w_in, w_out, final_g):
    t, d = x.shape
    f = w_out.shape[0]
    tm = _largest_tile(t, 512, 16)
    th = _largest_tile(f, 512, 128)
    nj = f // th
    final_norm = final_g is not None
    in_specs = [
        pl.BlockSpec((tm, d), lambda i, j: (i, 0)),
        pl.BlockSpec((tm, d), lambda i, j: (i, 0)),
        pl.BlockSpec((None, N_MOD, d), lambda i, j: (mod_row(i, tm), 0, 0)),
        pl.BlockSpec((d, th), lambda i, j: (0, j)),
        pl.BlockSpec((d, th), lambda i, j: (0, j + nj)),
        pl.BlockSpec((th, d), lambda i, j: (j, 0)),
    ]
    args = [x, h, mods, w_in, w_in, w_out]
    if final_norm:
        in_specs.append(pl.BlockSpec((1, d), lambda i, j: (0, 0)))
        args.append(final_g.reshape(1, d))
    est = 4 * tm * d * 4 + 2 * tm * d * 2 + 6 * d * th * 2 + 5 * tm * th * 4
    return pl.pallas_call(
        functools.partial(_ffn_kernel, n_hidden_blocks=nj, final_norm=final_norm),
        grid=(t // tm, nj),
        in_specs=in_specs,
        out_specs=pl.BlockSpec((tm, d), lambda i, j: (i, 0)),
        out_shape=jax.ShapeDtypeStruct((t, d), jnp.float32),
        compiler_params=_params(("parallel", "arbitrary"), est),
        name="ffn",
    )(*args)


def _pool_kernel(x_ref, xp_ref, xn_ref, mod_ref, g_ref, gf_ref, w_ref, sc_ref, o_ref, h2_ref,
                 h_sc, p_sc, d_sc, *, seq_len, blocks_per_seq):
    tm, d = x_ref.shape
    gd = d // len(POOL_WINDOWS)
    n = tm + 2 * POOL_HALO
    i = pl.program_id(0)
    blk = i % blocks_per_seq
    g = g_ref[...]
    shift = mod_ref[0:1, :]
    scale = mod_ref[1:2, :]
    x = x_ref[...]
    has_prev = (blk > 0).astype(jnp.float32)
    has_next = (blk < blocks_per_seq - 1).astype(jnp.float32)
    h_sc[0:POOL_HALO, :] = _norm_mod(xp_ref[...], g, shift, scale) * has_prev
    h_sc[POOL_HALO:POOL_HALO + tm, :] = _norm_mod(x, g, shift, scale)
    h_sc[POOL_HALO + tm:n, :] = _norm_mod(xn_ref[...], g, shift, scale) * has_next
    h_sc[n:, :] = jnp.zeros((POOL_HALO, d), jnp.float32)
    p_sc[n:, :] = jnp.zeros((POOL_HALO, gd), jnp.float32)

    pos = blk * tm + lax.broadcasted_iota(jnp.int32, (tm, 1), 0)
    for gi, win in enumerate(POOL_WINDOWS):
        cols = slice(gi * gd, (gi + 1) * gd)
        half = win // 2
        if half == 1:
            lo = h_sc[POOL_HALO - 1:POOL_HALO - 1 + tm, cols]
            hi = h_sc[POOL_HALO:POOL_HALO + tm, cols]
        else:
            p_sc[0:n, :] = h_sc[0:n, cols] + h_sc[1:n + 1, cols]
            s = 2
            while s < half:
                p_sc[0:n, :] = p_sc[0:n, :] + p_sc[s:n + s, :]
                s *= 2
            lo = p_sc[POOL_HALO - half:POOL_HALO - half + tm, :]
            hi = p_sc[POOL_HALO:POOL_HALO + tm, :]
        cnt = jnp.minimum(pos + half, seq_len) - jnp.maximum(pos - half, 0)
        dev = (lo + hi) / cnt.astype(jnp.float32) - h_sc[POOL_HALO:POOL_HALO + tm, cols]
        d_sc[:, cols] = _dot(dev.astype(MXU_DTYPE), w_ref[gi])
    x_new = x + d_sc[...] * (mod_ref[2:3, :] * sc_ref[...])
    o_ref[...] = x_new
    _emit_ffn_input(h2_ref, x_new, gf_ref, mod_ref)


def _pool_mixer(x, mods, mod_row, g, gf, w, scale, seq_len):
    t, d = x.shape
    tm = _largest_tile(seq_len, 512, 16)
    bps = seq_len // tm
    hb = tm // POOL_HALO
    n_halo_blocks = t // POOL_HALO
    gd = d // len(POOL_WINDOWS)
    est = 4 * tm * d * 4 + 2 * tm * d * 2 + 2 * (tm + 3 * POOL_HALO) * d * 4 + 2 * w.size * 2 \
        + 6 * tm * gd * 4
    row_spec = pl.BlockSpec((tm, d), lambda i: (i, 0))
    vec_spec = pl.BlockSpec((1, d), lambda i: (0, 0))
    return pl.pallas_call(
        functools.partial(_pool_kernel, seq_len=seq_len, blocks_per_seq=bps),
        grid=(t // tm,),
        in_specs=[
            row_spec,
            pl.BlockSpec((POOL_HALO, d), lambda i: (jnp.maximum(i * hb - 1, 0), 0)),
            pl.BlockSpec((POOL_HALO, d), lambda i: (jnp.minimum((i + 1) * hb, n_halo_blocks - 1), 0)),
            pl.BlockSpec((None, N_MOD, d), lambda i: (mod_row(i, tm), 0, 0)),
            vec_spec,
            vec_spec,
            pl.BlockSpec(w.shape, lambda i: (0, 0, 0)),
            vec_spec,
        ],
        out_specs=[row_spec, row_spec],
        out_shape=[jax.ShapeDtypeStruct((t, d), jnp.float32), jax.ShapeDtypeStruct((t, d), MXU_DTYPE)],
        scratch_shapes=[pltpu.VMEM((tm + 3 * POOL_HALO, d), jnp.float32),
                        pltpu.VMEM((tm + 3 * POOL_HALO, gd), jnp.float32),
                        pltpu.VMEM((tm, d), jnp.float32)],
        compiler_params=_params(("parallel",), est),
        name="pool_mixer",
    )(x, x, x, mods, g.reshape(1, d), gf.reshape(1, d), w, scale.reshape(1, d))


def _sgu_kernel(x_ref, mod_ref, g_ref, gf_ref, win_ref, ng_ref, ws_ref, bs_ref, wout_ref,
                o_ref, h2_ref, h_sc, v_sc, p_sc):
    tm, d = x_ref.shape
    sd = wout_ref.shape[0]
    gd = sd // SGU_GROUPS
    x = x_ref[...]
    h_sc[...] = _norm_mod(x, g_ref[...], mod_ref[0:1, :], mod_ref[1:2, :]).astype(MXU_DTYPE)
    h = h_sc[...]

    def gelu(z):
        return 0.5 * z * (1.0 + lax.erf(z * (1.0 / math.sqrt(2.0))))

    ssq = jnp.zeros((tm, 1), jnp.float32)
    for gi in range(SGU_GROUPS):
        v = gelu(_dot(h, win_ref[:, sd + gi * gd:sd + (gi + 1) * gd]))
        ssq = ssq + jnp.sum(v * v, axis=-1, keepdims=True)
        v_sc[:, gi * gd:(gi + 1) * gd] = v
    rstd = lax.rsqrt(ssq * (1.0 / sd) + EPS)

    for gi in range(SGU_GROUPS):
        cols = slice(gi * gd, (gi + 1) * gd)
        vn = (v_sc[:, cols] * rstd * ng_ref[:, cols]).astype(MXU_DTYPE)
        u = gelu(_dot(h, win_ref[:, cols]))
        wsg = ws_ref[gi]
        bias = bs_ref[gi]
        for c in range(tm // SGU_CHUNK):
            rows = slice(c * SGU_CHUNK, (c + 1) * SGU_CHUNK)
            vm = _dot(wsg, vn[rows, :]) + bias
            p_sc[rows, cols] = (u[rows, :] * vm).astype(MXU_DTYPE)
    x_new = x + mod_ref[2:3, :] * _dot(p_sc[...], wout_ref[...])
    o_ref[...] = x_new
    _emit_ffn_input(h2_ref, x_new, gf_ref, mod_ref)


def _sgu_mixer(x, mods, mod_row, g, gf, w_in, norm_g, ws, bs, w_out):
    t, d = x.shape
    sd = w_out.shape[0]
    tm = _largest_tile(t, 256, SGU_CHUNK)
    est = (w_in.size + w_out.size + ws.size) * 2 + 4 * tm * d * 4 + tm * sd * 4 + 4 * tm * d * 2 \
        + 6 * tm * (sd // SGU_GROUPS) * 4
    const = dict(pipeline_mode=pl.Buffered(1))
    row_spec = pl.BlockSpec((tm, d), lambda i: (i, 0))
    vec_spec = pl.BlockSpec((1, d), lambda i: (0, 0))
    return pl.pallas_call(
        _sgu_kernel,
        grid=(t // tm,),
        in_specs=[
            row_spec,
            pl.BlockSpec((None, N_MOD, d), lambda i: (mod_row(i, tm), 0, 0)),
            vec_spec,
            vec_spec,
            pl.BlockSpec(w_in.shape, lambda i: (0, 0), **const),
            pl.BlockSpec((1, sd), lambda i: (0, 0)),
            pl.BlockSpec(ws.shape, lambda i: (0, 0, 0), **const),
            pl.BlockSpec(bs.shape, lambda i: (0, 0, 0), **const),
            pl.BlockSpec(w_out.shape, lambda i: (0, 0), **const),
        ],
        out_specs=[row_spec, row_spec],
        out_shape=[jax.ShapeDtypeStruct((t, d), jnp.float32), jax.ShapeDtypeStruct((t, d), MXU_DTYPE)],
        scratch_shapes=[pltpu.VMEM((tm, d), MXU_DTYPE), pltpu.VMEM((tm, sd), jnp.float32),
                        pltpu.VMEM((tm, sd), MXU_DTYPE)],
        compiler_params=_params(("parallel",), est),
        name="sgu_mixer",
    )(x, mods, g.reshape(1, d), gf.reshape(1, d), w_in, norm_g.reshape(1, sd), ws, bs, w_out)


def _rope_tables(n_tokens):
    rows = n_tokens // GRID_W
    row = jnp.broadcast_to(jnp.arange(rows, dtype=jnp.float32)[:, None], (rows, GRID_W)).reshape(-1)
    col = jnp.broadcast_to(jnp.arange(GRID_W, dtype=jnp.float32)[None, :], (rows, GRID_W)).reshape(-1)
    inv = ROPE_BASE ** (-jnp.arange(ROPE_HALF, dtype=jnp.float32) / ROPE_HALF)
    ang = jnp.stack([row[:, None] * inv, col[:, None] * inv], axis=1)
    cos, sin = jnp.cos(ang), jnp.sin(ang)
    c = jnp.stack([cos, cos], axis=2).reshape(n_tokens, HEAD_DIM)
    s = jnp.stack([-sin, sin], axis=2).reshape(n_tokens, HEAD_DIM)
    return c, s


def _qkv_kernel(x_ref, mod_ref, g_ref, w_ref, *rest, rope, keep_f32, q_scale):
    rest = list(rest)
    if rope:
        c_ref, s_ref = rest[:2]
        rest = rest[2:]
    q_ref, k_ref, v_ref = rest[:3]
    rest = rest[3:]
    if keep_f32:
        kf_ref, vf_ref = rest[:2]
        rest = rest[2:]
    (h_sc,) = rest
    tm, d = x_ref.shape
    j = pl.program_id(1)

    @pl.when(j == 0)
    def _():
        h = _norm_mod(x_ref[...], g_ref[...], mod_ref[0:1, :], mod_ref[1:2, :])
        h_sc[...] = h.astype(MXU_DTYPE)

    def rotate(dst_ref, y, mult):
        c = c_ref[...]
        s = s_ref[...]
        lane = lax.broadcasted_iota(jnp.int32, (tm, HEAD_DIM), 1)
        low_half = (lane % (2 * ROPE_HALF)) < ROPE_HALF
        for gi in range(d // HEAD_DIM):
            cols = slice(gi * HEAD_DIM, (gi + 1) * HEAD_DIM)
            yg = y[:, cols]
            partner = jnp.where(low_half,
                                pltpu.roll(yg, HEAD_DIM - ROPE_HALF, 1),
                                pltpu.roll(yg, ROPE_HALF, 1))
            r = yg * c + partner * s
            if mult != 1.0:
                r = r * mult
            dst_ref[:, cols] = r.astype(dst_ref.dtype)

    @pl.when(j == 0)
    def _():
        y = _dot(h_sc[...], w_ref[...])
        if rope:
            rotate(q_ref, y, q_scale)
        else:
            q_ref[...] = (y * q_scale).astype(q_ref.dtype)

    @pl.when(j == 1)
    def _():
        y = _dot(h_sc[...], w_ref[...])
        if keep_f32:
            kf_ref[...] = y
        if rope:
            rotate(k_ref, y, 1.0)
        else:
            k_ref[...] = y.astype(k_ref.dtype)

    @pl.when(j == 2)
    def _():
        y = _dot(h_sc[...], w_ref[...])
        if keep_f32:
            vf_ref[...] = y
        v_ref[...] = y.astype(v_ref.dtype)


def _qkv(x, mods, mod_row, g, w_qkv, seq_len, rope, keep_f32):
    t, d = x.shape
    tm = _largest_tile(seq_len, 512, 16)
    bps = seq_len // tm
    q_scale = HEAD_DIM ** -0.5 * LOG2E
    in_specs = [
        pl.BlockSpec((tm, d), lambda i, j: (i, 0)),
        pl.BlockSpec((None, N_MOD, d), lambda i, j: (mod_row(i, tm), 0, 0)),
        pl.BlockSpec((1, d), lambda i, j: (0, 0)),
        pl.BlockSpec((d, d), lambda i, j: (0, j)),
    ]
    args = [x, mods, g.reshape(1, d), w_qkv]
    if rope:
        c, s = _rope_tables(seq_len)
        in_specs += [pl.BlockSpec((tm, HEAD_DIM), lambda i, j: (i % bps, 0))] * 2
        args += [c, s]
    row_spec = pl.BlockSpec((tm, d), lambda i, j: (i, 0))
    out_specs = [row_spec] * 3
    out_shape = [jax.ShapeDtypeStruct((t, d), MXU_DTYPE)] * 3
    if keep_f32:
        out_specs += [row_spec] * 2
        out_shape += [jax.ShapeDtypeStruct((t, d), jnp.float32)] * 2
    est = 2 * tm * d * 4 + 2 * d * d * 2 + tm * d * 2 + 6 * tm * d * 2 + 3 * tm * d * 4 \
        + (4 * tm * d * 4 if keep_f32 else 0)
    return pl.pallas_call(
        functools.partial(_qkv_kernel, rope=rope, keep_f32=keep_f32, q_scale=q_scale),
        grid=(t // tm, 3),
        in_specs=in_specs,
        out_specs=out_specs,
        out_shape=out_shape,
        scratch_shapes=[pltpu.VMEM((tm, d), MXU_DTYPE)],
        compiler_params=_params(("parallel", "arbitrary"), est),
        name="attn_qkv",
    )(*args)


def _attn_kernel(q_ref, k_ref, v_ref, *rest, lam_init, has_cache):
    if has_cache:
        ck_ref, cv_ref, lp_ref, sg_ref, o_ref = rest
        cached_v = cv_ref[...].astype(MXU_DTYPE)
    else:
        lp_ref, sg_ref, o_ref = rest
    n_new = k_ref.shape[0]
    chunk = min(KV_CHUNK, n_new)
    lp = lp_ref[...]
    lam = (jnp.exp(jnp.sum(lp[0:1, :] * lp[1:2, :], axis=-1, keepdims=True))
           - jnp.exp(jnp.sum(lp[2:3, :] * lp[3:4, :], axis=-1, keepdims=True)) + lam_init)
    outs = []
    for mp in range(2):
        cols = slice(mp * HEAD_DIM, (mp + 1) * HEAD_DIM)
        q = q_ref[:, cols]
        kv = []
        if has_cache:
            kv.append((ck_ref[:, cols].astype(MXU_DTYPE), cached_v))
        for c0 in range(0, n_new, chunk):
            kv.append((k_ref[c0:c0 + chunk, cols], v_ref[c0:c0 + chunk, :]))
        m = denom = acc = None
        for kc, vc in kv:
            s = lax.dot_general(q, kc, (((1,), (1,)), ((), ())), preferred_element_type=jnp.float32)
            mc = jnp.max(s, axis=-1, keepdims=True)
            if m is None:
                m = mc
                p = jnp.exp2(s - m)
                denom = jnp.sum(p, axis=-1, keepdims=True)
                acc = _dot(p.astype(MXU_DTYPE), vc)
            else:
                m_new = jnp.maximum(m, mc)
                alpha = jnp.exp2(m - m_new)
                p = jnp.exp2(s - m_new)
                denom = alpha * denom + jnp.sum(p, axis=-1, keepdims=True)
                acc = alpha * acc + _dot(p.astype(MXU_DTYPE), vc)
                m = m_new
        outs.append(acc / denom)
    o = outs[0] - lam * outs[1]
    ms = jnp.mean(o * o, axis=-1, keepdims=True)
    o = (o * lax.rsqrt(ms + EPS) * sg_ref[...]) * (1.0 - lam_init)
    o_ref[...] = o.astype(o_ref.dtype)


def _attention(q, k, v, cache, lam_params, subln_g, lam_init):
    b, n, d = q.shape
    heads = d // V_DIM
    tq = _largest_tile(n, 512, 16)
    in_specs = [
        pl.BlockSpec((None, tq, V_DIM), lambda bi, hi, qi: (bi, qi, hi)),
        pl.BlockSpec((None, n, V_DIM), lambda bi, hi, qi: (bi, 0, hi)),
        pl.BlockSpec((None, n, V_DIM), lambda bi, hi, qi: (bi, 0, hi)),
    ]
    args = [q, k, v]
    past = 0
    if cache is not None:
        cache_k, cache_v, slot = cache
        past = cache_k.shape[2]
        in_specs += [pl.BlockSpec((None, None, past, V_DIM), lambda bi, hi, qi: (bi, slot, 0, hi))] * 2
        args += [cache_k, cache_v]
    in_specs += [
        pl.BlockSpec(lam_params.shape, lambda bi, hi, qi: (0, 0)),
        pl.BlockSpec((1, V_DIM), lambda bi, hi, qi: (0, 0)),
    ]
    args += [lam_params, subln_g.reshape(1, V_DIM)]
    chunk = min(KV_CHUNK, n)
    est = 4 * tq * V_DIM * 2 + 4 * n * V_DIM * 2 + 4 * past * V_DIM * 4 \
        + 6 * tq * chunk * 4 + 8 * tq * V_DIM * 4
    return pl.pallas_call(
        functools.partial(_attn_kernel, lam_init=lam_init, has_cache=cache is not None),
        grid=(b, heads, n // tq),
        in_specs=in_specs,
        out_specs=pl.BlockSpec((None, tq, V_DIM), lambda bi, hi, qi: (bi, qi, hi)),
        out_shape=jax.ShapeDtypeStruct((b, n, d), MXU_DTYPE),
        compiler_params=_params(("parallel", "parallel", "parallel"), est),
        name="diff_attention",
    )(*args)


def _proj_kernel(a_ref, w_ref, x_ref, mod_ref, gf_ref, o_ref, h2_ref):
    x_new = x_ref[...] + mod_ref[2:3, :] * _dot(a_ref[...], w_ref[...])
    o_ref[...] = x_new
    _emit_ffn_input(h2_ref, x_new, gf_ref, mod_ref)


def _attn_out_proj(a, w_o, x, mods, mod_row, gf):
    t, d = x.shape
    tm = _largest_tile(t, 512, 16)
    est = 4 * tm * d * 2 + 2 * d * d * 2 + 6 * tm * d * 4
    row_spec = pl.BlockSpec((tm, d), lambda i: (i, 0))
    return pl.pallas_call(
        _proj_kernel,
        grid=(t // tm,),
        in_specs=[
            row_spec,
            pl.BlockSpec((d, d), lambda i: (0, 0)),
            row_spec,
            pl.BlockSpec((None, N_MOD, d), lambda i: (mod_row(i, tm), 0, 0)),
            pl.BlockSpec((1, d), lambda i: (0, 0)),
        ],
        out_specs=[row_spec, row_spec],
        out_shape=[jax.ShapeDtypeStruct((t, d), jnp.float32), jax.ShapeDtypeStruct((t, d), MXU_DTYPE)],
        compiler_params=_params(("parallel",), est),
        name="attn_out_proj",
    )(a, w_o, x, mods, gf.reshape(1, d))


def kernel(x_prompt, x_sample, cache_k, cache_v, c, c_ctx, ada_w, ada_b, norm_mix_g, norm_ffn_g,
           pool_w, pool_scale, sgu_w_in, sgu_norm_g, sgu_ws, sgu_b, sgu_w_out,
           attn_w_qkv, attn_lambda, attn_subln_g, attn_w_o, ffn_w_in, ffn_w_out, final_g):
    batch, seq, d = x_prompt.shape
    dec_batch, dec_seq, _ = x_sample.shape
    depth = ada_w.shape[0]
    assert dec_batch < COND_ROWS and d % V_DIM == 0 and dec_seq % GRID_W == 0

    cond = jnp.zeros((COND_ROWS, d), jnp.float32).at[:dec_batch].set(c).at[dec_batch].set(c_ctx)
    mods = _ada_mods(cond, ada_w, ada_b).reshape(depth, COND_ROWS, N_MOD, d)

    cast = lambda w: w.astype(MXU_DTYPE)
    cache_k = cache_k.reshape(cache_k.shape[:3] + (d,))
    cache_v = cache_v.reshape(cache_v.shape[:3] + (d,))
    streams = {
        "ctx": (x_prompt.reshape(batch * seq, d), seq, lambda i, tm: dec_batch),
        "lat": (x_sample.reshape(dec_batch * dec_seq, d), dec_seq,
                lambda i, tm: (i * tm) // dec_seq),
    }
    xs = {name: s[0] for name, s in streams.items()}
    new_k, new_v = [], []
    for i in range(depth):
        kind, slot = i % N_MIXERS, i // N_MIXERS
        m_i = mods[i]
        gf = norm_ffn_g[i]
        for name, (_, n, mod_row) in streams.items():
            x = xs[name]
            if kind == 0:
                x, h2 = _pool_mixer(x, m_i, mod_row, norm_mix_g[i], gf, cast(pool_w[slot]),
                                    pool_scale[slot], n)
            elif kind == 1:
                x, h2 = _sgu_mixer(x, m_i, mod_row, norm_mix_g[i], gf, cast(sgu_w_in[slot]),
                                   sgu_norm_g[slot], cast(sgu_ws[slot]), sgu_b[slot][:, :, None],
                                   cast(sgu_w_out[slot]))
            else:
                lam_init = 0.8 - 0.6 * math.exp(-0.3 * i)
                is_ctx = name == "ctx"
                outs = _qkv(x, m_i, mod_row, norm_mix_g[i], cast(attn_w_qkv[slot]), n,
                            rope=not is_ctx, keep_f32=is_ctx)
                nb = x.shape[0] // n
                q, k, v = (a.reshape(nb, n, d) for a in outs[:3])
                if is_ctx:
                    new_k.append(outs[3])
                    new_v.append(outs[4])
                a = _attention(q, k, v, None if is_ctx else (cache_k, cache_v, slot),
                               attn_lambda[slot], attn_subln_g[slot], lam_init)
                x, h2 = _attn_out_proj(a.reshape(-1, d), cast(attn_w_o[slot]), x, m_i, mod_row, gf)
            xs[name] = _ffn(x, h2, m_i, mod_row, cast(ffn_w_in[i]), cast(ffn_w_out[i]),
                            final_g if i == depth - 1 else None)
    heads = d // V_DIM
    y_prompt = xs["ctx"].reshape(batch, seq, d)
    y_sample = xs["lat"].reshape(dec_batch, dec_seq, d)
    new_cache_k = jnp.stack([a.reshape(batch, seq, heads, 2, HEAD_DIM) for a in new_k], axis=1)
    new_cache_v = jnp.stack([a.reshape(batch, seq, heads, V_DIM) for a in new_v], axis=1)
    return (y_prompt, y_sample, new_cache_k, new_cache_v)
```

```python
import functools
import math

import jax
import jax.numpy as jnp
from jax import lax
from jax.experimental import pallas as pl
from jax.experimental.pallas import tpu as pltpu

EPS = 1e-6
N_MOD = 6
N_MIXERS = 3
POOL_WINDOWS = (2, 4, 8, 16)
POOL_HALO = 8
SGU_GROUPS = 8
SGU_CHUNK = 128
HEAD_DIM = 128
V_DIM = 2 * HEAD_DIM
GRID_W = 64
ROPE_BASE = 10000.0
ROPE_HALF = HEAD_DIM // 4
LOG2E = 1.4426950408889634
KV_CHUNK = 512
Q_TILE = 1024

MXU_DTYPE = jnp.bfloat16
COND_ROWS = 16
V7X_VMEM_BYTES = 64 * 1024 * 1024
VMEM_CAP_BYTES = V7X_VMEM_BYTES - 8 * 1024 * 1024


def _vmem_limit(estimate_bytes):
    return int(min(max(estimate_bytes * 5 // 4, 16 * 1024 * 1024), VMEM_CAP_BYTES))


def _params(semantics, vmem_estimate):
    return pltpu.CompilerParams(dimension_semantics=semantics,
                                vmem_limit_bytes=_vmem_limit(vmem_estimate))


def _dot(a, b):
    return jnp.dot(a, b, preferred_element_type=jnp.float32)


def _norm_mod(x, g, shift, scale):
    ms = jnp.mean(x * x, axis=-1, keepdims=True)
    return (x * lax.rsqrt(ms + EPS)) * (g * (1.0 + scale)) + shift


def _emit_ffn_input(h2_ref, x_new, gf_ref, mod_ref):
    h2_ref[...] = _norm_mod(x_new, gf_ref[...], mod_ref[3:4, :], mod_ref[4:5, :]).astype(h2_ref.dtype)


def _largest_tile(n, cap, quantum):
    t = min(cap, n)
    t -= t % quantum
    while n % t:
        t -= quantum
    return t


def _ada_kernel(cond_ref, w_ref, b_ref, o_ref):
    c = cond_ref[...]
    s = (c * jax.nn.sigmoid(c)).astype(MXU_DTYPE)
    o_ref[...] = _dot(s, w_ref[...].astype(MXU_DTYPE)) + b_ref[...]


def _ada_mods(cond, ada_w, ada_b):
    depth, d, n = ada_w.shape
    bn = _largest_tile(n, 1024, 128)
    est = 2 * d * bn * 4 + d * bn * 2 + 4 * COND_ROWS * (d + bn) * 4
    return pl.pallas_call(
        _ada_kernel,
        grid=(depth, n // bn),
        in_specs=[
            pl.BlockSpec((COND_ROWS, d), lambda l, j: (0, 0)),
            pl.BlockSpec((None, d, bn), lambda l, j: (l, 0, j)),
            pl.BlockSpec((None, 1, bn), lambda l, j: (l, 0, j)),
        ],
        out_specs=pl.BlockSpec((None, COND_ROWS, bn), lambda l, j: (l, 0, j)),
        out_shape=jax.ShapeDtypeStruct((depth, COND_ROWS, n), jnp.float32),
        compiler_params=_params(("parallel", "parallel"), est),
        name="ada_mods",
    )(cond, ada_w, ada_b.reshape(depth, 1, n))


def _ffn_kernel(x_ref, h_ref, mod_ref, wa_ref, wb_ref, wo_ref, *rest, n_hidden_blocks, final_norm):
    if final_norm:
        fg_ref, o_ref = rest
    else:
        (o_ref,) = rest
    j = pl.program_id(1)

    @pl.when(j == 0)
    def _():
        o_ref[...] = jnp.zeros_like(o_ref)

    h = h_ref[...]
    a = _dot(h, wa_ref[...])
    b = _dot(h, wb_ref[...])
    act = (a * jax.nn.sigmoid(a)) * b
    o_ref[...] += _dot(act.astype(MXU_DTYPE), wo_ref[...])

    @pl.when(j == n_hidden_blocks - 1)
    def _():
        y = x_ref[...] + mod_ref[5:6, :] * o_ref[...]
        if final_norm:
            ms = jnp.mean(y * y, axis=-1, keepdims=True)
            y = y * lax.rsqrt(ms + EPS) * fg_ref[...]
        o_ref[...] = y


def _ffn(x, h, mods, mod_row, w_in, w_out, layer, final_g):
    t, d = x.shape
    f = w_out.shape[1]
    tm = _largest_tile(t, 512, 16)
    th = _largest_tile(f, 512, 128)
    nj = f // th
    final_norm = final_g is not None
    in_specs = [
        pl.BlockSpec((tm, d), lambda i, j: (i, 0)),
        pl.BlockSpec((tm, d), lambda i, j: (i, 0)),
        pl.BlockSpec((None, N_MOD, d), lambda i, j: (mod_row(i, tm), 0, 0)),
        pl.BlockSpec((None, d, th), lambda i, j: (layer, 0, j)),
        pl.BlockSpec((None, d, th), lambda i, j: (layer, 0, j + nj)),
        pl.BlockSpec((None, th, d), lambda i, j: (layer, j, 0)),
    ]
    args = [x, h, mods, w_in, w_in, w_out]
    if final_norm:
        in_specs.append(pl.BlockSpec((1, d), lambda i, j: (0, 0)))
        args.append(final_g.reshape(1, d))
    est = 4 * tm * d * 4 + 2 * tm * d * 2 + 6 * d * th * 2 + 5 * tm * th * 4
    return pl.pallas_call(
        functools.partial(_ffn_kernel, n_hidden_blocks=nj, final_norm=final_norm),
        grid=(t // tm, nj),
        in_specs=in_specs,
        out_specs=pl.BlockSpec((tm, d), lambda i, j: (i, 0)),
        out_shape=jax.ShapeDtypeStruct((t, d), jnp.float32),
        compiler_params=_params(("parallel", "arbitrary"), est),
        name="ffn",
    )(*args)


def _pool_kernel(x_ref, xp_ref, xn_ref, mod_ref, g_ref, gf_ref, w_ref, sc_ref, o_ref, h2_ref,
                 h_sc, p_sc, d_sc, *, seq_len, blocks_per_seq):
    tm, d = x_ref.shape
    gd = d // len(POOL_WINDOWS)
    n = tm + 2 * POOL_HALO
    i = pl.program_id(0)
    blk = i % blocks_per_seq
    g = g_ref[...]
    shift = mod_ref[0:1, :]
    scale = mod_ref[1:2, :]
    x = x_ref[...]
    has_prev = (blk > 0).astype(jnp.float32)
    has_next = (blk < blocks_per_seq - 1).astype(jnp.float32)
    h_sc[0:POOL_HALO, :] = _norm_mod(xp_ref[...], g, shift, scale) * has_prev
    h_sc[POOL_HALO:POOL_HALO + tm, :] = _norm_mod(x, g, shift, scale)
    h_sc[POOL_HALO + tm:n, :] = _norm_mod(xn_ref[...], g, shift, scale) * has_next
    h_sc[n:, :] = jnp.zeros((POOL_HALO, d), jnp.float32)
    p_sc[n:, :] = jnp.zeros((POOL_HALO, gd), jnp.float32)

    pos = blk * tm + lax.broadcasted_iota(jnp.int32, (tm, 1), 0)
    for gi, win in enumerate(POOL_WINDOWS):
        cols = slice(gi * gd, (gi + 1) * gd)
        half = win // 2
        if half == 1:
            lo = h_sc[POOL_HALO - 1:POOL_HALO - 1 + tm, cols]
            hi = h_sc[POOL_HALO:POOL_HALO + tm, cols]
        else:
            p_sc[0:n, :] = h_sc[0:n, cols] + h_sc[1:n + 1, cols]
            s = 2
            while s < half:
                p_sc[0:n, :] = p_sc[0:n, :] + p_sc[s:n + s, :]
                s *= 2
            lo = p_sc[POOL_HALO - half:POOL_HALO - half + tm, :]
            hi = p_sc[POOL_HALO:POOL_HALO + tm, :]
        cnt = jnp.minimum(pos + half, seq_len) - jnp.maximum(pos - half, 0)
        inv_cnt = 1.0 / cnt.astype(jnp.float32)
        dev = (lo + hi) * inv_cnt - h_sc[POOL_HALO:POOL_HALO + tm, cols]
        d_sc[:, cols] = _dot(dev.astype(MXU_DTYPE), w_ref[gi])
    x_new = x + d_sc[...] * (mod_ref[2:3, :] * sc_ref[...])
    o_ref[...] = x_new
    _emit_ffn_input(h2_ref, x_new, gf_ref, mod_ref)


def _pool_mixer(x, mods, mod_row, g, gf, w, scale, seq_len):
    t, d = x.shape
    tm = _largest_tile(seq_len, 512, 16)
    bps = seq_len // tm
    hb = tm // POOL_HALO
    n_halo_blocks = t // POOL_HALO
    gd = d // len(POOL_WINDOWS)
    est = 4 * tm * d * 4 + 2 * tm * d * 2 + 2 * (tm + 3 * POOL_HALO) * d * 4 + 2 * w.size * 2 \
        + 6 * tm * gd * 4
    row_spec = pl.BlockSpec((tm, d), lambda i: (i, 0))
    vec_spec = pl.BlockSpec((1, d), lambda i: (0, 0))
    return pl.pallas_call(
        functools.partial(_pool_kernel, seq_len=seq_len, blocks_per_seq=bps),
        grid=(t // tm,),
        in_specs=[
            row_spec,
            pl.BlockSpec((POOL_HALO, d), lambda i: (jnp.maximum(i * hb - 1, 0), 0)),
            pl.BlockSpec((POOL_HALO, d), lambda i: (jnp.minimum((i + 1) * hb, n_halo_blocks - 1), 0)),
            pl.BlockSpec((None, N_MOD, d), lambda i: (mod_row(i, tm), 0, 0)),
            vec_spec,
            vec_spec,
            pl.BlockSpec(w.shape, lambda i: (0, 0, 0)),
            vec_spec,
        ],
        out_specs=[row_spec, row_spec],
        out_shape=[jax.ShapeDtypeStruct((t, d), jnp.float32), jax.ShapeDtypeStruct((t, d), MXU_DTYPE)],
        scratch_shapes=[pltpu.VMEM((tm + 3 * POOL_HALO, d), jnp.float32),
                        pltpu.VMEM((tm + 3 * POOL_HALO, gd), jnp.float32),
                        pltpu.VMEM((tm, d), jnp.float32)],
        compiler_params=_params(("parallel",), est),
        name="pool_mixer",
    )(x, x, x, mods, g.reshape(1, d), gf.reshape(1, d), w, scale.reshape(1, d))


def _sgu_kernel(x_ref, mod_ref, g_ref, gf_ref, win_ref, ng_ref, ws_ref, bs_ref, wout_ref,
                o_ref, h2_ref, h_sc, v_sc, p_sc):
    tm, d = x_ref.shape
    sd = wout_ref.shape[0]
    gd = sd // SGU_GROUPS
    x = x_ref[...]
    h_sc[...] = _norm_mod(x, g_ref[...], mod_ref[0:1, :], mod_ref[1:2, :]).astype(MXU_DTYPE)
    h = h_sc[...]

    def gelu(z):
        return 0.5 * z * (1.0 + lax.erf(z * (1.0 / math.sqrt(2.0))))

    ssq = jnp.zeros((tm, 1), jnp.float32)
    for gi in range(SGU_GROUPS):
        v = gelu(_dot(h, win_ref[:, sd + gi * gd:sd + (gi + 1) * gd]))
        ssq = ssq + jnp.sum(v * v, axis=-1, keepdims=True)
        v_sc[:, gi * gd:(gi + 1) * gd] = v
    rstd = lax.rsqrt(ssq * (1.0 / sd) + EPS)

    for gi in range(SGU_GROUPS):
        cols = slice(gi * gd, (gi + 1) * gd)
        vn = (v_sc[:, cols] * rstd * ng_ref[:, cols]).astype(MXU_DTYPE)
        u = gelu(_dot(h, win_ref[:, cols]))
        wsg = ws_ref[gi]
        bias = bs_ref[gi]
        for c in range(tm // SGU_CHUNK):
            rows = slice(c * SGU_CHUNK, (c + 1) * SGU_CHUNK)
            vm = _dot(wsg, vn[rows, :]) + bias
            p_sc[rows, cols] = (u[rows, :] * vm).astype(MXU_DTYPE)
    x_new = x + mod_ref[2:3, :] * _dot(p_sc[...], wout_ref[...])
    o_ref[...] = x_new
    _emit_ffn_input(h2_ref, x_new, gf_ref, mod_ref)


def _sgu_mixer(x, mods, mod_row, g, gf, w_in, norm_g, ws, bs, w_out):
    t, d = x.shape
    sd = w_out.shape[0]
    tm = _largest_tile(t, 256, SGU_CHUNK)
    est = (w_in.size + w_out.size + ws.size) * 2 + 4 * tm * d * 4 + tm * sd * 4 + 4 * tm * d * 2 \
        + 6 * tm * (sd // SGU_GROUPS) * 4
    const = dict(pipeline_mode=pl.Buffered(1))
    row_spec = pl.BlockSpec((tm, d), lambda i: (i, 0))
    vec_spec = pl.BlockSpec((1, d), lambda i: (0, 0))
    return pl.pallas_call(
        _sgu_kernel,
        grid=(t // tm,),
        in_specs=[
            row_spec,
            pl.BlockSpec((None, N_MOD, d), lambda i: (mod_row(i, tm), 0, 0)),
            vec_spec,
            vec_spec,
            pl.BlockSpec(w_in.shape, lambda i: (0, 0), **const),
            pl.BlockSpec((1, sd), lambda i: (0, 0)),
            pl.BlockSpec(ws.shape, lambda i: (0, 0, 0), **const),
            pl.BlockSpec(bs.shape, lambda i: (0, 0, 0), **const),
            pl.BlockSpec(w_out.shape, lambda i: (0, 0), **const),
        ],
        out_specs=[row_spec, row_spec],
        out_shape=[jax.ShapeDtypeStruct((t, d), jnp.float32), jax.ShapeDtypeStruct((t, d), MXU_DTYPE)],
        scratch_shapes=[pltpu.VMEM((tm, d), MXU_DTYPE), pltpu.VMEM((tm, sd), jnp.float32),
                        pltpu.VMEM((tm, sd), MXU_DTYPE)],
        compiler_params=_params(("parallel",), est),
        name="sgu_mixer",
    )(x, mods, g.reshape(1, d), gf.reshape(1, d), w_in, norm_g.reshape(1, sd), ws, bs, w_out)


def _rope_tables(n_tokens):
    rows = n_tokens // GRID_W
    row = jnp.broadcast_to(jnp.arange(rows, dtype=jnp.float32)[:, None], (rows, GRID_W)).reshape(-1)
    col = jnp.broadcast_to(jnp.arange(GRID_W, dtype=jnp.float32)[None, :], (rows, GRID_W)).reshape(-1)
    inv = ROPE_BASE ** (-jnp.arange(ROPE_HALF, dtype=jnp.float32) / ROPE_HALF)
    ang = jnp.stack([row[:, None] * inv, col[:, None] * inv], axis=1)
    cos, sin = jnp.cos(ang), jnp.sin(ang)
    c = jnp.stack([cos, cos], axis=2).reshape(n_tokens, HEAD_DIM)
    s = jnp.stack([-sin, sin], axis=2).reshape(n_tokens, HEAD_DIM)
    return c, s


def _qkv_kernel(x_ref, mod_ref, g_ref, w_ref, *rest, rope, keep_f32, q_scale):
    rest = list(rest)
    if rope:
        c_ref, s_ref = rest[:2]
        rest = rest[2:]
    q_ref, k_ref, v_ref = rest[:3]
    rest = rest[3:]
    if keep_f32:
        kf_ref, vf_ref = rest[:2]
        rest = rest[2:]
    (h_sc,) = rest
    tm, d = x_ref.shape
    j = pl.program_id(1)

    @pl.when(j == 0)
    def _():
        h = _norm_mod(x_ref[...], g_ref[...], mod_ref[0:1, :], mod_ref[1:2, :])
        h_sc[...] = h.astype(MXU_DTYPE)

    def rotate(dst_ref, y, mult):
        c = c_ref[...]
        s = s_ref[...]
        lane = lax.broadcasted_iota(jnp.int32, (tm, HEAD_DIM), 1)
        low_half = (lane % (2 * ROPE_HALF)) < ROPE_HALF
        for gi in range(d // HEAD_DIM):
            cols = slice(gi * HEAD_DIM, (gi + 1) * HEAD_DIM)
            yg = y[:, cols]
            partner = jnp.where(low_half,
                                pltpu.roll(yg, HEAD_DIM - ROPE_HALF, 1),
                                pltpu.roll(yg, ROPE_HALF, 1))
            r = yg * c + partner * s
            if mult != 1.0:
                r = r * mult
            dst_ref[:, cols] = r.astype(dst_ref.dtype)

    @pl.when(j == 0)
    def _():
        y = _dot(h_sc[...], w_ref[...])
        if rope:
            rotate(q_ref, y, q_scale)
        else:
            q_ref[...] = (y * q_scale).astype(q_ref.dtype)

    @pl.when(j == 1)
    def _():
        y = _dot(h_sc[...], w_ref[...])
        if keep_f32:
            kf_ref[...] = y
        if rope:
            rotate(k_ref, y, 1.0)
        else:
            k_ref[...] = y.astype(k_ref.dtype)

    @pl.when(j == 2)
    def _():
        y = _dot(h_sc[...], w_ref[...])
        if keep_f32:
            vf_ref[...] = y
        v_ref[...] = y.astype(v_ref.dtype)


def _qkv(x, mods, mod_row, g, w_qkv, seq_len, rope, keep_f32):
    t, d = x.shape
    tm = _largest_tile(seq_len, 512, 16)
    bps = seq_len // tm
    q_scale = HEAD_DIM ** -0.5 * LOG2E
    in_specs = [
        pl.BlockSpec((tm, d), lambda i, j: (i, 0)),
        pl.BlockSpec((None, N_MOD, d), lambda i, j: (mod_row(i, tm), 0, 0)),
        pl.BlockSpec((1, d), lambda i, j: (0, 0)),
        pl.BlockSpec((d, d), lambda i, j: (0, j)),
    ]
    args = [x, mods, g.reshape(1, d), w_qkv]
    if rope:
        c, s = _rope_tables(seq_len)
        in_specs += [pl.BlockSpec((tm, HEAD_DIM), lambda i, j: (i % bps, 0))] * 2
        args += [c, s]
    row_spec = pl.BlockSpec((tm, d), lambda i, j: (i, 0))
    out_specs = [row_spec] * 3
    out_shape = [jax.ShapeDtypeStruct((t, d), MXU_DTYPE)] * 3
    if keep_f32:
        out_specs += [row_spec] * 2
        out_shape += [jax.ShapeDtypeStruct((t, d), jnp.float32)] * 2
    est = 2 * tm * d * 4 + 2 * d * d * 2 + tm * d * 2 + 6 * tm * d * 2 + 3 * tm * d * 4 \
        + (4 * tm * d * 4 if keep_f32 else 0)
    return pl.pallas_call(
        functools.partial(_qkv_kernel, rope=rope, keep_f32=keep_f32, q_scale=q_scale),
        grid=(t // tm, 3),
        in_specs=in_specs,
        out_specs=out_specs,
        out_shape=out_shape,
        scratch_shapes=[pltpu.VMEM((tm, d), MXU_DTYPE)],
        compiler_params=_params(("parallel", "arbitrary"), est),
        name="attn_qkv",
    )(*args)


def _attn_kernel(q_ref, k_ref, v_ref, *rest, lam_init, has_cache):
    if has_cache:
        ck_ref, cv_ref, lp_ref, sg_ref, o_ref = rest
        cached_v = cv_ref[...].astype(MXU_DTYPE)
    else:
        lp_ref, sg_ref, o_ref = rest
    n_new = k_ref.shape[0]
    chunk = min(KV_CHUNK, n_new)
    lp = lp_ref[...]
    lam = (jnp.exp(jnp.sum(lp[0:1, :] * lp[1:2, :], axis=-1, keepdims=True))
           - jnp.exp(jnp.sum(lp[2:3, :] * lp[3:4, :], axis=-1, keepdims=True)) + lam_init)
    for hd in range(q_ref.shape[1] // V_DIM):
        vcols = slice(hd * V_DIM, (hd + 1) * V_DIM)
        outs = []
        for mp in range(2):
            cols = slice(hd * V_DIM + mp * HEAD_DIM, hd * V_DIM + (mp + 1) * HEAD_DIM)
            q = q_ref[:, cols]
            kv = []
            if has_cache:
                kv.append((ck_ref[:, cols].astype(MXU_DTYPE), cached_v[:, vcols]))
            for c0 in range(0, n_new, chunk):
                kv.append((k_ref[c0:c0 + chunk, cols], v_ref[c0:c0 + chunk, vcols]))
            m = denom = acc = None
            for kc, vc in kv:
                s = lax.dot_general(q, kc, (((1,), (1,)), ((), ())), preferred_element_type=jnp.float32)
                mc = jnp.max(s, axis=-1, keepdims=True)
                if m is None:
                    m = mc
                    p = jnp.exp2(s - m)
                    denom = jnp.sum(p, axis=-1, keepdims=True)
                    acc = _dot(p.astype(MXU_DTYPE), vc)
                else:
                    m_new = jnp.maximum(m, mc)
                    alpha = jnp.exp2(m - m_new)
                    p = jnp.exp2(s - m_new)
                    denom = alpha * denom + jnp.sum(p, axis=-1, keepdims=True)
                    acc = alpha * acc + _dot(p.astype(MXU_DTYPE), vc)
                    m = m_new
            outs.append(acc / denom)
        o = outs[0] - lam * outs[1]
        ms = jnp.mean(o * o, axis=-1, keepdims=True)
        o = (o * lax.rsqrt(ms + EPS) * sg_ref[...]) * (1.0 - lam_init)
        o_ref[:, vcols] = o.astype(o_ref.dtype)


def _attention(q, k, v, cache, lam_params, subln_g, lam_init):
    b, n, d = q.shape
    heads = d // V_DIM
    tq = _largest_tile(n, Q_TILE, 16)
    hw = V_DIM
    in_specs = [
        pl.BlockSpec((None, tq, hw), lambda bi, hi, qi: (bi, qi, hi)),
        pl.BlockSpec((None, n, hw), lambda bi, hi, qi: (bi, 0, hi)),
        pl.BlockSpec((None, n, hw), lambda bi, hi, qi: (bi, 0, hi)),
    ]
    args = [q, k, v]
    past = 0
    if cache is not None:
        cache_k, cache_v, slot = cache
        past = cache_k.shape[2]
        in_specs += [pl.BlockSpec((None, None, past, hw), lambda bi, hi, qi: (bi, slot, 0, hi))] * 2
        args += [cache_k, cache_v]
    in_specs += [
        pl.BlockSpec(lam_params.shape, lambda bi, hi, qi: (0, 0)),
        pl.BlockSpec((1, V_DIM), lambda bi, hi, qi: (0, 0)),
    ]
    args += [lam_params, subln_g.reshape(1, V_DIM)]
    chunk = min(KV_CHUNK, n)
    est = 4 * tq * hw * 2 + 4 * n * hw * 2 + 4 * past * hw * 4 \
        + 6 * tq * chunk * 4 + 8 * tq * V_DIM * 4
    return pl.pallas_call(
        functools.partial(_attn_kernel, lam_init=lam_init, has_cache=cache is not None),
        grid=(b, heads, n // tq),
        in_specs=in_specs,
        out_specs=pl.BlockSpec((None, tq, hw), lambda bi, hi, qi: (bi, qi, hi)),
        out_shape=jax.ShapeDtypeStruct((b, n, d), MXU_DTYPE),
        compiler_params=_params(("parallel", "parallel", "parallel"), est),
        name="diff_attention",
    )(*args)


def _proj_kernel(a_ref, w_ref, x_ref, mod_ref, gf_ref, o_ref, h2_ref):
    x_new = x_ref[...] + mod_ref[2:3, :] * _dot(a_ref[...], w_ref[...])
    o_ref[...] = x_new
    _emit_ffn_input(h2_ref, x_new, gf_ref, mod_ref)


def _attn_out_proj(a, w_o, x, mods, mod_row, gf):
    t, d = x.shape
    tm = _largest_tile(t, 512, 16)
    est = 4 * tm * d * 2 + 2 * d * d * 2 + 6 * tm * d * 4
    row_spec = pl.BlockSpec((tm, d), lambda i: (i, 0))
    return pl.pallas_call(
        _proj_kernel,
        grid=(t // tm,),
        in_specs=[
            row_spec,
            pl.BlockSpec((d, d), lambda i: (0, 0)),
            row_spec,
            pl.BlockSpec((None, N_MOD, d), lambda i: (mod_row(i, tm), 0, 0)),
            pl.BlockSpec((1, d), lambda i: (0, 0)),
        ],
        out_specs=[row_spec, row_spec],
        out_shape=[jax.ShapeDtypeStruct((t, d), jnp.float32), jax.ShapeDtypeStruct((t, d), MXU_DTYPE)],
        compiler_params=_params(("parallel",), est),
        name="attn_out_proj",
    )(a, w_o, x, mods, gf.reshape(1, d))


def kernel(x_prompt, x_sample, cache_k, cache_v, c, c_ctx, ada_w, ada_b, norm_mix_g, norm_ffn_g,
           pool_w, pool_scale, sgu_w_in, sgu_norm_g, sgu_ws, sgu_b, sgu_w_out,
           attn_w_qkv, attn_lambda, attn_subln_g, attn_w_o, ffn_w_in, ffn_w_out, final_g):
    batch, seq, d = x_prompt.shape
    dec_batch, dec_seq, _ = x_sample.shape
    depth = ada_w.shape[0]
    assert dec_batch < COND_ROWS and d % V_DIM == 0 and dec_seq % GRID_W == 0

    cond = jnp.zeros((COND_ROWS, d), jnp.float32).at[:dec_batch].set(c).at[dec_batch].set(c_ctx)
    mods = _ada_mods(cond, ada_w, ada_b).reshape(depth, COND_ROWS, N_MOD, d)

    cast = lambda w: w.astype(MXU_DTYPE)
    ffn_w_in, ffn_w_out = cast(ffn_w_in), cast(ffn_w_out)
    cache_k = cache_k.reshape(cache_k.shape[:3] + (d,))
    cache_v = cache_v.reshape(cache_v.shape[:3] + (d,))
    streams = {
        "ctx": (x_prompt.reshape(batch * seq, d), seq, lambda i, tm: dec_batch),
        "lat": (x_sample.reshape(dec_batch * dec_seq, d), dec_seq,
                lambda i, tm: (i * tm) // dec_seq),
    }
    xs = {name: s[0] for name, s in streams.items()}
    new_k, new_v = [], []
    for i in range(depth):
        kind, slot = i % N_MIXERS, i // N_MIXERS
        m_i = mods[i]
        gf = norm_ffn_g[i]
        for name, (_, n, mod_row) in streams.items():
            x = xs[name]
            if kind == 0:
                x, h2 = _pool_mixer(x, m_i, mod_row, norm_mix_g[i], gf, cast(pool_w[slot]),
                                    pool_scale[slot], n)
            elif kind == 1:
                x, h2 = _sgu_mixer(x, m_i, mod_row, norm_mix_g[i], gf, cast(sgu_w_in[slot]),
                                   sgu_norm_g[slot], cast(sgu_ws[slot]), sgu_b[slot][:, :, None],
                                   cast(sgu_w_out[slot]))
            else:
                lam_init = 0.8 - 0.6 * math.exp(-0.3 * i)
                is_ctx = name == "ctx"
                outs = _qkv(x, m_i, mod_row, norm_mix_g[i], cast(attn_w_qkv[slot]), n,
                            rope=not is_ctx, keep_f32=is_ctx)
                nb = x.shape[0] // n
                q, k, v = (a.reshape(nb, n, d) for a in outs[:3])
                if is_ctx:
                    new_k.append(outs[3])
                    new_v.append(outs[4])
                a = _attention(q, k, v, None if is_ctx else (cache_k, cache_v, slot),
                               attn_lambda[slot], attn_subln_g[slot], lam_init)
                x, h2 = _attn_out_proj(a.reshape(-1, d), cast(attn_w_o[slot]), x, m_i, mod_row, gf)
            xs[name] = _ffn(x, h2, m_i, mod_row, ffn_w_in, ffn_w_out, i,
                            final_g if i == depth - 1 else None)
    heads = d // V_DIM
    y_prompt = xs["ctx"].reshape(batch, seq, d)
    y_sample = xs["lat"].reshape(dec_batch, dec_seq, d)
    new_cache_k = jnp.stack([a.reshape(batch, seq, heads, 2, HEAD_DIM) for a in new_k], axis=1)
    new_cache_v = jnp.stack([a.reshape(batch, seq, heads, V_DIM) for a in new_v], axis=1)
    return (y_prompt, y_sample, new_cache_k, new_cache_v)
```

```python
import functools
import math

import jax
import jax.numpy as jnp
from jax import lax
from jax.experimental import pallas as pl
from jax.experimental.pallas import tpu as pltpu

EPS = 1e-6
N_MOD = 6
N_MIXERS = 3
POOL_WINDOWS = (2, 4, 8, 16)
POOL_HALO = 8
SGU_GROUPS = 8
SGU_CHUNK = 128
HEAD_DIM = 128
V_DIM = 2 * HEAD_DIM
GRID_W = 64
ROPE_BASE = 10000.0
ROPE_HALF = HEAD_DIM // 4
LOG2E = 1.4426950408889634
KV_CHUNK = 512
Q_TILE = 1024

MXU_DTYPE = jnp.bfloat16
COND_ROWS = 16
V7X_VMEM_BYTES = 64 * 1024 * 1024
VMEM_CAP_BYTES = V7X_VMEM_BYTES - 8 * 1024 * 1024


def _vmem_limit(estimate_bytes):
    return int(min(max(estimate_bytes * 5 // 4, 16 * 1024 * 1024), VMEM_CAP_BYTES))


def _params(semantics, vmem_estimate):
    return pltpu.CompilerParams(dimension_semantics=semantics,
                                vmem_limit_bytes=_vmem_limit(vmem_estimate))


def _dot(a, b):
    return jnp.dot(a, b, preferred_element_type=jnp.float32)


def _norm_mod(x, g, shift, scale):
    ms = jnp.mean(x * x, axis=-1, keepdims=True)
    return (x * lax.rsqrt(ms + EPS)) * (g * (1.0 + scale)) + shift


def _emit_ffn_input(h2_ref, x_new, gf_ref, mod_ref):
    h2_ref[...] = _norm_mod(x_new, gf_ref[...], mod_ref[3:4, :], mod_ref[4:5, :]).astype(h2_ref.dtype)


def _split_pending(refs, pending, n=1):
    xs, rest = refs[:n], refs[n:]
    if not pending:
        return xs, None, None, rest
    return xs, rest[:n], rest[n], rest[n + 1:]


def _stream_rows(x_ref, f_ref, modp_ref):
    x = x_ref[...]
    if f_ref is not None:
        x = x + modp_ref[5:6, :] * f_ref[...]
    return x


def _largest_tile(n, cap, quantum):
    t = min(cap, n)
    t -= t % quantum
    while n % t:
        t -= quantum
    return t


def _ada_kernel(cond_ref, w_ref, b_ref, o_ref):
    c = cond_ref[...]
    s = (c * jax.nn.sigmoid(c)).astype(MXU_DTYPE)
    o_ref[...] = _dot(s, w_ref[...].astype(MXU_DTYPE)) + b_ref[...]


def _ada_mods(cond, ada_w, ada_b):
    depth, d, n = ada_w.shape
    bn = _largest_tile(n, 1024, 128)
    est = 2 * d * bn * 4 + d * bn * 2 + 4 * COND_ROWS * (d + bn) * 4
    return pl.pallas_call(
        _ada_kernel,
        grid=(depth, n // bn),
        in_specs=[
            pl.BlockSpec((COND_ROWS, d), lambda l, j: (0, 0)),
            pl.BlockSpec((None, d, bn), lambda l, j: (l, 0, j)),
            pl.BlockSpec((None, 1, bn), lambda l, j: (l, 0, j)),
        ],
        out_specs=pl.BlockSpec((None, COND_ROWS, bn), lambda l, j: (l, 0, j)),
        out_shape=jax.ShapeDtypeStruct((depth, COND_ROWS, n), jnp.float32),
        compiler_params=_params(("parallel", "parallel"), est),
        name="ada_mods",
    )(cond, ada_w, ada_b.reshape(depth, 1, n))


def _ffn_kernel(*refs, n_hidden_blocks, residual, final_norm):
    refs = list(refs)
    if residual:
        x_ref, mod_ref = refs[:2]
        refs = refs[2:]
    h_ref, wa_ref, wb_ref, wo_ref = refs[:4]
    refs = refs[4:]
    if final_norm:
        fg_ref = refs.pop(0)
    (o_ref,) = refs
    j = pl.program_id(1)

    @pl.when(j == 0)
    def _():
        o_ref[...] = jnp.zeros_like(o_ref)

    h = h_ref[...]
    a = _dot(h, wa_ref[...])
    b = _dot(h, wb_ref[...])
    act = (a * jax.nn.sigmoid(a)) * b
    o_ref[...] += _dot(act.astype(MXU_DTYPE), wo_ref[...])

    if residual:
        @pl.when(j == n_hidden_blocks - 1)
        def _():
            y = x_ref[...] + mod_ref[5:6, :] * o_ref[...]
            if final_norm:
                ms = jnp.mean(y * y, axis=-1, keepdims=True)
                y = y * lax.rsqrt(ms + EPS) * fg_ref[...]
            o_ref[...] = y


def _ffn(h, w_in, w_out, layer, residual=None, final_g=None):
    t, d = h.shape
    f = w_out.shape[1]
    fused = residual is not None
    tm = _largest_tile(t, 512 if fused else 1024, 16)
    th = _largest_tile(f, 512, 128)
    nj = f // th
    row_spec = pl.BlockSpec((tm, d), lambda i, j: (i, 0))
    in_specs, args = [], []
    if fused:
        x, mods, mod_row = residual
        in_specs += [row_spec, pl.BlockSpec((None, N_MOD, d), lambda i, j: (mod_row(i, tm), 0, 0))]
        args += [x, mods]
    in_specs += [
        row_spec,
        pl.BlockSpec((None, d, th), lambda i, j: (layer, 0, j)),
        pl.BlockSpec((None, d, th), lambda i, j: (layer, 0, j + nj)),
        pl.BlockSpec((None, th, d), lambda i, j: (layer, j, 0)),
    ]
    args += [h, w_in, w_in, w_out]
    if final_g is not None:
        in_specs.append(pl.BlockSpec((1, d), lambda i, j: (0, 0)))
        args.append(final_g.reshape(1, d))
    est = (4 if fused else 2) * tm * d * 4 + 2 * tm * d * 2 + 6 * d * th * 2 + 7 * tm * th * 4
    return pl.pallas_call(
        functools.partial(_ffn_kernel, n_hidden_blocks=nj, residual=fused, final_norm=final_g is not None),
        grid=(t // tm, nj),
        in_specs=in_specs,
        out_specs=row_spec,
        out_shape=jax.ShapeDtypeStruct((t, d), jnp.float32),
        compiler_params=_params(("parallel", "arbitrary"), est),
        name="ffn",
    )(*args)


def _pool_kernel(*refs, seq_len, blocks_per_seq, pending):
    (x_ref, xp_ref, xn_ref), fs, modp_ref, rest = _split_pending(list(refs), pending, 3)
    f_ref, fp_ref, fn_ref = fs if pending else (None, None, None)
    mod_ref, g_ref, gf_ref, w_ref, sc_ref, o_ref, h2_ref, h_sc, p_sc, d_sc = rest
    tm, d = x_ref.shape
    gd = d // len(POOL_WINDOWS)
    n = tm + 2 * POOL_HALO
    i = pl.program_id(0)
    blk = i % blocks_per_seq
    g = g_ref[...]
    shift = mod_ref[0:1, :]
    scale = mod_ref[1:2, :]
    x = _stream_rows(x_ref, f_ref, modp_ref)
    has_prev = (blk > 0).astype(jnp.float32)
    has_next = (blk < blocks_per_seq - 1).astype(jnp.float32)
    h_sc[0:POOL_HALO, :] = _norm_mod(_stream_rows(xp_ref, fp_ref, modp_ref), g, shift, scale) * has_prev
    h_sc[POOL_HALO:POOL_HALO + tm, :] = _norm_mod(x, g, shift, scale)
    h_sc[POOL_HALO + tm:n, :] = _norm_mod(_stream_rows(xn_ref, fn_ref, modp_ref), g, shift, scale) * has_next
    h_sc[n:, :] = jnp.zeros((POOL_HALO, d), jnp.float32)
    p_sc[n:, :] = jnp.zeros((POOL_HALO, gd), jnp.float32)

    pos = blk * tm + lax.broadcasted_iota(jnp.int32, (tm, 1), 0)
    for gi, win in enumerate(POOL_WINDOWS):
        cols = slice(gi * gd, (gi + 1) * gd)
        half = win // 2
        if half == 1:
            lo = h_sc[POOL_HALO - 1:POOL_HALO - 1 + tm, cols]
            hi = h_sc[POOL_HALO:POOL_HALO + tm, cols]
        else:
            p_sc[0:n, :] = h_sc[0:n, cols] + h_sc[1:n + 1, cols]
            s = 2
            while s < half:
                p_sc[0:n, :] = p_sc[0:n, :] + p_sc[s:n + s, :]
                s *= 2
            lo = p_sc[POOL_HALO - half:POOL_HALO - half + tm, :]
            hi = p_sc[POOL_HALO:POOL_HALO + tm, :]
        cnt = jnp.minimum(pos + half, seq_len) - jnp.maximum(pos - half, 0)
        inv_cnt = 1.0 / cnt.astype(jnp.float32)
        dev = (lo + hi) * inv_cnt - h_sc[POOL_HALO:POOL_HALO + tm, cols]
        d_sc[:, cols] = _dot(dev.astype(MXU_DTYPE), w_ref[gi])
    x_new = x + d_sc[...] * (mod_ref[2:3, :] * sc_ref[...])
    o_ref[...] = x_new
    _emit_ffn_input(h2_ref, x_new, gf_ref, mod_ref)


def _pool_mixer(x, pending, mods, mod_row, g, gf, w, scale, seq_len):
    t, d = x.shape
    tm = _largest_tile(seq_len, 512, 16)
    bps = seq_len // tm
    hb = tm // POOL_HALO
    n_halo_blocks = t // POOL_HALO
    gd = d // len(POOL_WINDOWS)
    row_spec = pl.BlockSpec((tm, d), lambda i: (i, 0))
    vec_spec = pl.BlockSpec((1, d), lambda i: (0, 0))
    mod_spec = pl.BlockSpec((None, N_MOD, d), lambda i: (mod_row(i, tm), 0, 0))
    halo_specs = [
        row_spec,
        pl.BlockSpec((POOL_HALO, d), lambda i: (jnp.maximum(i * hb - 1, 0), 0)),
        pl.BlockSpec((POOL_HALO, d), lambda i: (jnp.minimum((i + 1) * hb, n_halo_blocks - 1), 0)),
    ]
    in_specs, args = list(halo_specs), [x, x, x]
    if pending:
        f, mods_prev = pending
        in_specs += halo_specs + [mod_spec]
        args += [f, f, f, mods_prev]
    in_specs += [mod_spec, vec_spec, vec_spec, pl.BlockSpec(w.shape, lambda i: (0, 0, 0)), vec_spec]
    args += [mods, g.reshape(1, d), gf.reshape(1, d), w, scale.reshape(1, d)]
    est = (6 if pending else 4) * tm * d * 4 + 2 * tm * d * 2 + 2 * (tm + 3 * POOL_HALO) * d * 4 \
        + 2 * w.size * 2 + 6 * tm * gd * 4
    return pl.pallas_call(
        functools.partial(_pool_kernel, seq_len=seq_len, blocks_per_seq=bps, pending=bool(pending)),
        grid=(t // tm,),
        in_specs=in_specs,
        out_specs=[row_spec, row_spec],
        out_shape=[jax.ShapeDtypeStruct((t, d), jnp.float32), jax.ShapeDtypeStruct((t, d), MXU_DTYPE)],
        scratch_shapes=[pltpu.VMEM((tm + 3 * POOL_HALO, d), jnp.float32),
                        pltpu.VMEM((tm + 3 * POOL_HALO, gd), jnp.float32),
                        pltpu.VMEM((tm, d), jnp.float32)],
        compiler_params=_params(("parallel",), est),
        name="pool_mixer",
    )(*args)


def _sgu_kernel(*refs, pending):
    (x_ref,), fs, modp_ref, rest = _split_pending(list(refs), pending)
    mod_ref, g_ref, gf_ref, win_ref, ng_ref, ws_ref, bs_ref, wout_ref, o_ref, h2_ref, h_sc, v_sc, p_sc = rest
    tm, d = x_ref.shape
    sd = wout_ref.shape[0]
    gd = sd // SGU_GROUPS
    x = _stream_rows(x_ref, fs[0] if pending else None, modp_ref)
    h_sc[...] = _norm_mod(x, g_ref[...], mod_ref[0:1, :], mod_ref[1:2, :]).astype(MXU_DTYPE)
    h = h_sc[...]

    def gelu(z):
        return 0.5 * z * (1.0 + lax.erf(z * (1.0 / math.sqrt(2.0))))

    ssq = jnp.zeros((tm, 1), jnp.float32)
    for gi in range(SGU_GROUPS):
        v = gelu(_dot(h, win_ref[:, sd + gi * gd:sd + (gi + 1) * gd]))
        ssq = ssq + jnp.sum(v * v, axis=-1, keepdims=True)
        v_sc[:, gi * gd:(gi + 1) * gd] = v
    rstd = lax.rsqrt(ssq * (1.0 / sd) + EPS)

    for gi in range(SGU_GROUPS):
        cols = slice(gi * gd, (gi + 1) * gd)
        vn = (v_sc[:, cols] * rstd * ng_ref[:, cols]).astype(MXU_DTYPE)
        u = gelu(_dot(h, win_ref[:, cols]))
        wsg = ws_ref[gi]
        bias = bs_ref[gi]
        for c in range(tm // SGU_CHUNK):
            rows = slice(c * SGU_CHUNK, (c + 1) * SGU_CHUNK)
            vm = _dot(wsg, vn[rows, :]) + bias
            p_sc[rows, cols] = (u[rows, :] * vm).astype(MXU_DTYPE)
    x_new = x + mod_ref[2:3, :] * _dot(p_sc[...], wout_ref[...])
    o_ref[...] = x_new
    _emit_ffn_input(h2_ref, x_new, gf_ref, mod_ref)


def _sgu_mixer(x, pending, mods, mod_row, g, gf, w_in, norm_g, ws, bs, w_out):
    t, d = x.shape
    sd = w_out.shape[0]
    tm = _largest_tile(t, 256, SGU_CHUNK)
    est = (w_in.size + w_out.size + ws.size) * 2 + (6 if pending else 4) * tm * d * 4 + tm * sd * 4 \
        + 4 * tm * d * 2 + 6 * tm * (sd // SGU_GROUPS) * 4
    const = dict(pipeline_mode=pl.Buffered(1))
    row_spec = pl.BlockSpec((tm, d), lambda i: (i, 0))
    vec_spec = pl.BlockSpec((1, d), lambda i: (0, 0))
    mod_spec = pl.BlockSpec((None, N_MOD, d), lambda i: (mod_row(i, tm), 0, 0))
    in_specs, args = [row_spec], [x]
    if pending:
        in_specs += [row_spec, mod_spec]
        args += list(pending)
    in_specs += [
        mod_spec,
        vec_spec,
        vec_spec,
        pl.BlockSpec(w_in.shape, lambda i: (0, 0), **const),
        pl.BlockSpec((1, sd), lambda i: (0, 0)),
        pl.BlockSpec(ws.shape, lambda i: (0, 0, 0), **const),
        pl.BlockSpec(bs.shape, lambda i: (0, 0, 0), **const),
        pl.BlockSpec(w_out.shape, lambda i: (0, 0), **const),
    ]
    args += [mods, g.reshape(1, d), gf.reshape(1, d), w_in, norm_g.reshape(1, sd), ws, bs, w_out]
    return pl.pallas_call(
        functools.partial(_sgu_kernel, pending=bool(pending)),
        grid=(t // tm,),
        in_specs=in_specs,
        out_specs=[row_spec, row_spec],
        out_shape=[jax.ShapeDtypeStruct((t, d), jnp.float32), jax.ShapeDtypeStruct((t, d), MXU_DTYPE)],
        scratch_shapes=[pltpu.VMEM((tm, d), MXU_DTYPE), pltpu.VMEM((tm, sd), jnp.float32),
                        pltpu.VMEM((tm, sd), MXU_DTYPE)],
        compiler_params=_params(("parallel",), est),
        name="sgu_mixer",
    )(*args)


def _rope_tables(n_tokens):
    rows = n_tokens // GRID_W
    row = jnp.broadcast_to(jnp.arange(rows, dtype=jnp.float32)[:, None], (rows, GRID_W)).reshape(-1)
    col = jnp.broadcast_to(jnp.arange(GRID_W, dtype=jnp.float32)[None, :], (rows, GRID_W)).reshape(-1)
    inv = ROPE_BASE ** (-jnp.arange(ROPE_HALF, dtype=jnp.float32) / ROPE_HALF)
    ang = jnp.stack([row[:, None] * inv, col[:, None] * inv], axis=1)
    cos, sin = jnp.cos(ang), jnp.sin(ang)
    c = jnp.stack([cos, cos], axis=2).reshape(n_tokens, HEAD_DIM)
    s = jnp.stack([-sin, sin], axis=2).reshape(n_tokens, HEAD_DIM)
    return c, s


def _qkv_kernel(*refs, pending, rope, keep_f32, q_scale):
    (x_ref,), fs, modp_ref, rest = _split_pending(list(refs), pending)
    mod_ref, g_ref, w_ref = rest[:3]
    rest = rest[3:]
    if rope:
        c_ref, s_ref = rest[:2]
        rest = rest[2:]
    q_ref, k_ref, v_ref = rest[:3]
    rest = rest[3:]
    if keep_f32:
        kf_ref, vf_ref = rest[:2]
        rest = rest[2:]
    (h_sc,) = rest
    tm, d = x_ref.shape
    j = pl.program_id(1)

    @pl.when(j == 0)
    def _():
        x = _stream_rows(x_ref, fs[0] if pending else None, modp_ref)
        h = _norm_mod(x, g_ref[...], mod_ref[0:1, :], mod_ref[1:2, :])
        h_sc[...] = h.astype(MXU_DTYPE)

    def rotate(dst_ref, y, mult):
        c = c_ref[...]
        s = s_ref[...]
        lane = lax.broadcasted_iota(jnp.int32, (tm, HEAD_DIM), 1)
        low_half = (lane % (2 * ROPE_HALF)) < ROPE_HALF
        for gi in range(d // HEAD_DIM):
            cols = slice(gi * HEAD_DIM, (gi + 1) * HEAD_DIM)
            yg = y[:, cols]
            partner = jnp.where(low_half,
                                pltpu.roll(yg, HEAD_DIM - ROPE_HALF, 1),
                                pltpu.roll(yg, ROPE_HALF, 1))
            r = yg * c + partner * s
            if mult != 1.0:
                r = r * mult
            dst_ref[:, cols] = r.astype(dst_ref.dtype)

    @pl.when(j == 0)
    def _():
        y = _dot(h_sc[...], w_ref[...])
        if rope:
            rotate(q_ref, y, q_scale)
        else:
            q_ref[...] = (y * q_scale).astype(q_ref.dtype)

    @pl.when(j == 1)
    def _():
        y = _dot(h_sc[...], w_ref[...])
        if keep_f32:
            kf_ref[...] = y
        if rope:
            rotate(k_ref, y, 1.0)
        else:
            k_ref[...] = y.astype(k_ref.dtype)

    @pl.when(j == 2)
    def _():
        y = _dot(h_sc[...], w_ref[...])
        if keep_f32:
            vf_ref[...] = y
        v_ref[...] = y.astype(v_ref.dtype)


def _qkv(x, pending, mods, mod_row, g, w_qkv, seq_len, rope, keep_f32):
    t, d = x.shape
    tm = _largest_tile(seq_len, 256 if keep_f32 else 512, 16)
    bps = seq_len // tm
    q_scale = HEAD_DIM ** -0.5 * LOG2E
    row_spec = pl.BlockSpec((tm, d), lambda i, j: (i, 0))
    mod_spec = pl.BlockSpec((None, N_MOD, d), lambda i, j: (mod_row(i, tm), 0, 0))
    in_specs, args = [row_spec], [x]
    if pending:
        in_specs += [row_spec, mod_spec]
        args += list(pending)
    in_specs += [mod_spec, pl.BlockSpec((1, d), lambda i, j: (0, 0)), pl.BlockSpec((d, d), lambda i, j: (0, j))]
    args += [mods, g.reshape(1, d), w_qkv]
    if rope:
        c, s = _rope_tables(seq_len)
        in_specs += [pl.BlockSpec((tm, HEAD_DIM), lambda i, j: (i % bps, 0))] * 2
        args += [c, s]
    out_specs = [row_spec] * 3
    out_shape = [jax.ShapeDtypeStruct((t, d), MXU_DTYPE)] * 3
    if keep_f32:
        out_specs += [row_spec] * 2
        out_shape += [jax.ShapeDtypeStruct((t, d), jnp.float32)] * 2
    est = (4 if pending else 2) * tm * d * 4 + 2 * d * d * 2 + tm * d * 2 + 6 * tm * d * 2 + 3 * tm * d * 4 \
        + (4 * tm * d * 4 if keep_f32 else 0)
    return pl.pallas_call(
        functools.partial(_qkv_kernel, pending=bool(pending), rope=rope, keep_f32=keep_f32, q_scale=q_scale),
        grid=(t // tm, 3),
        in_specs=in_specs,
        out_specs=out_specs,
        out_shape=out_shape,
        scratch_shapes=[pltpu.VMEM((tm, d), MXU_DTYPE)],
        compiler_params=_params(("parallel", "arbitrary"), est),
        name="attn_qkv",
    )(*args)


def _attn_kernel(q_ref, k_ref, v_ref, *rest, lam_init, has_cache):
    if has_cache:
        ck_ref, cv_ref, lp_ref, sg_ref, o_ref = rest
        cached_v = cv_ref[...].astype(MXU_DTYPE)
    else:
        lp_ref, sg_ref, o_ref = rest
    n_new = k_ref.shape[0]
    chunk = min(KV_CHUNK, n_new)
    lp = lp_ref[...]
    lam = (jnp.exp(jnp.sum(lp[0:1, :] * lp[1:2, :], axis=-1, keepdims=True))
           - jnp.exp(jnp.sum(lp[2:3, :] * lp[3:4, :], axis=-1, keepdims=True)) + lam_init)
    outs = []
    for mp in range(2):
        cols = slice(mp * HEAD_DIM, (mp + 1) * HEAD_DIM)
        q = q_ref[:, cols]
        kv = []
        if has_cache:
            kv.append((ck_ref[:, cols].astype(MXU_DTYPE), cached_v))
        for c0 in range(0, n_new, chunk):
            kv.append((k_ref[c0:c0 + chunk, cols], v_ref[c0:c0 + chunk, :]))
        m = denom = acc = None
        for kc, vc in kv:
            s = lax.dot_general(q, kc, (((1,), (1,)), ((), ())), preferred_element_type=jnp.float32)
            mc = jnp.max(s, axis=-1, keepdims=True)
            if m is None:
                m = mc
                p = jnp.exp2(s - m)
                denom = jnp.sum(p, axis=-1, keepdims=True)
                acc = _dot(p.astype(MXU_DTYPE), vc)
            else:
                m_new = jnp.maximum(m, mc)
                alpha = jnp.exp2(m - m_new)
                p = jnp.exp2(s - m_new)
                denom = alpha * denom + jnp.sum(p, axis=-1, keepdims=True)
                acc = alpha * acc + _dot(p.astype(MXU_DTYPE), vc)
                m = m_new
        outs.append(acc / denom)
    o = outs[0] - lam * outs[1]
    ms = jnp.mean(o * o, axis=-1, keepdims=True)
    o = (o * lax.rsqrt(ms + EPS) * sg_ref[...]) * (1.0 - lam_init)
    o_ref[...] = o.astype(o_ref.dtype)


def _attention(q, k, v, cache, lam_params, subln_g, lam_init):
    b, n, d = q.shape
    heads = d // V_DIM
    tq = _largest_tile(n, Q_TILE, 16)
    in_specs = [
        pl.BlockSpec((None, tq, V_DIM), lambda bi, hi, qi: (bi, qi, hi)),
        pl.BlockSpec((None, n, V_DIM), lambda bi, hi, qi: (bi, 0, hi)),
        pl.BlockSpec((None, n, V_DIM), lambda bi, hi, qi: (bi, 0, hi)),
    ]
    args = [q, k, v]
    past = 0
    if cache is not None:
        cache_k, cache_v, slot = cache
        past = cache_k.shape[2]
        in_specs += [pl.BlockSpec((None, None, past, V_DIM), lambda bi, hi, qi: (bi, slot, 0, hi))] * 2
        args += [cache_k, cache_v]
    in_specs += [
        pl.BlockSpec(lam_params.shape, lambda bi, hi, qi: (0, 0)),
        pl.BlockSpec((1, V_DIM), lambda bi, hi, qi: (0, 0)),
    ]
    args += [lam_params, subln_g.reshape(1, V_DIM)]
    chunk = min(KV_CHUNK, n)
    est = 4 * tq * V_DIM * 2 + 4 * n * V_DIM * 2 + 4 * past * V_DIM * 4 \
        + 6 * tq * chunk * 4 + 8 * tq * V_DIM * 4
    return pl.pallas_call(
        functools.partial(_attn_kernel, lam_init=lam_init, has_cache=cache is not None),
        grid=(b, heads, n // tq),
        in_specs=in_specs,
        out_specs=pl.BlockSpec((None, tq, V_DIM), lambda bi, hi, qi: (bi, qi, hi)),
        out_shape=jax.ShapeDtypeStruct((b, n, d), MXU_DTYPE),
        compiler_params=_params(("parallel", "parallel", "parallel"), est),
        name="diff_attention",
    )(*args)


def _proj_kernel(*refs, pending):
    (x_ref,), fs, modp_ref, rest = _split_pending(list(refs), pending)
    a_ref, w_ref, mod_ref, gf_ref, o_ref, h2_ref = rest
    x = _stream_rows(x_ref, fs[0] if pending else None, modp_ref)
    x_new = x + mod_ref[2:3, :] * _dot(a_ref[...], w_ref[...])
    o_ref[...] = x_new
    _emit_ffn_input(h2_ref, x_new, gf_ref, mod_ref)


def _attn_out_proj(x, pending, a, w_o, mods, mod_row, gf):
    t, d = x.shape
    tm = _largest_tile(t, 512, 16)
    est = 4 * tm * d * 2 + 2 * d * d * 2 + (8 if pending else 6) * tm * d * 4
    row_spec = pl.BlockSpec((tm, d), lambda i: (i, 0))
    mod_spec = pl.BlockSpec((None, N_MOD, d), lambda i: (mod_row(i, tm), 0, 0))
    in_specs, args = [row_spec], [x]
    if pending:
        in_specs += [row_spec, mod_spec]
        args += list(pending)
    in_specs += [row_spec, pl.BlockSpec((d, d), lambda i: (0, 0)), mod_spec, pl.BlockSpec((1, d), lambda i: (0, 0))]
    args += [a, w_o, mods, gf.reshape(1, d)]
    return pl.pallas_call(
        functools.partial(_proj_kernel, pending=bool(pending)),
        grid=(t // tm,),
        in_specs=in_specs,
        out_specs=[row_spec, row_spec],
        out_shape=[jax.ShapeDtypeStruct((t, d), jnp.float32), jax.ShapeDtypeStruct((t, d), MXU_DTYPE)],
        compiler_params=_params(("parallel",), est),
        name="attn_out_proj",
    )(*args)


def kernel(x_prompt, x_sample, cache_k, cache_v, c, c_ctx, ada_w, ada_b, norm_mix_g, norm_ffn_g,
           pool_w, pool_scale, sgu_w_in, sgu_norm_g, sgu_ws, sgu_b, sgu_w_out,
           attn_w_qkv, attn_lambda, attn_subln_g, attn_w_o, ffn_w_in, ffn_w_out, final_g):
    batch, seq, d = x_prompt.shape
    dec_batch, dec_seq, _ = x_sample.shape
    depth = ada_w.shape[0]
    assert dec_batch < COND_ROWS and d % V_DIM == 0 and dec_seq % GRID_W == 0

    cond = jnp.zeros((COND_ROWS, d), jnp.float32).at[:dec_batch].set(c).at[dec_batch].set(c_ctx)
    mods = _ada_mods(cond, ada_w, ada_b).reshape(depth, COND_ROWS, N_MOD, d)

    cast = lambda w: w.astype(MXU_DTYPE)
    ffn_w_in, ffn_w_out = cast(ffn_w_in), cast(ffn_w_out)
    cache_k = cache_k.reshape(cache_k.shape[:3] + (d,))
    cache_v = cache_v.reshape(cache_v.shape[:3] + (d,))
    streams = {
        "ctx": (x_prompt.reshape(batch * seq, d), seq, lambda i, tm: dec_batch),
        "lat": (x_sample.reshape(dec_batch * dec_seq, d), dec_seq,
                lambda i, tm: (i * tm) // dec_seq),
    }
    xs = {name: s[0] for name, s in streams.items()}
    pend = {name: None for name in streams}
    new_k, new_v = [], []
    for i in range(depth):
        kind, slot = i % N_MIXERS, i // N_MIXERS
        m_i = mods[i]
        gf = norm_ffn_g[i]
        last = i == depth - 1
        for name, (_, n, mod_row) in streams.items():
            x, pending = xs[name], pend[name]
            if kind == 0:
                x, h2 = _pool_mixer(x, pending, m_i, mod_row, norm_mix_g[i], gf, cast(pool_w[slot]),
                                    pool_scale[slot], n)
            elif kind == 1:
                x, h2 = _sgu_mixer(x, pending, m_i, mod_row, norm_mix_g[i], gf, cast(sgu_w_in[slot]),
                                   sgu_norm_g[slot], cast(sgu_ws[slot]), sgu_b[slot][:, :, None],
                                   cast(sgu_w_out[slot]))
            else:
                lam_init = 0.8 - 0.6 * math.exp(-0.3 * i)
                is_ctx = name == "ctx"
                outs = _qkv(x, pending, m_i, mod_row, norm_mix_g[i], cast(attn_w_qkv[slot]), n,
                            rope=not is_ctx, keep_f32=is_ctx)
                nb = x.shape[0] // n
                q, k, v = (a.reshape(nb, n, d) for a in outs[:3])
                if is_ctx:
                    new_k.append(outs[3])
                    new_v.append(outs[4])
                a = _attention(q, k, v, None if is_ctx else (cache_k, cache_v, slot),
                               attn_lambda[slot], attn_subln_g[slot], lam_init)
                x, h2 = _attn_out_proj(x, pending, a.reshape(-1, d), cast(attn_w_o[slot]), m_i, mod_row, gf)
            if last:
                xs[name] = _ffn(h2, ffn_w_in, ffn_w_out, i, residual=(x, m_i, mod_row), final_g=final_g)
                pend[name] = None
            else:
                xs[name] = x
                pend[name] = (_ffn(h2, ffn_w_in, ffn_w_out, i), m_i)
    heads = d // V_DIM
    y_prompt = xs["ctx"].reshape(batch, seq, d)
    y_sample = xs["lat"].reshape(dec_batch, dec_seq, d)
    new_cache_k = jnp.stack([a.reshape(batch, seq, heads, 2, HEAD_DIM) for a in new_k], axis=1)
    new_cache_v = jnp.stack([a.reshape(batch, seq, heads, V_DIM) for a in new_v], axis=1)
    return (y_prompt, y_sample, new_cache_k, new_cache_v)
```

```python
import functools
import math

import jax
import jax.numpy as jnp
from jax import lax
from jax.experimental import pallas as pl
from jax.experimental.pallas import tpu as pltpu

EPS = 1e-6
N_MOD = 6
N_MIXERS = 3
POOL_WINDOWS = (2, 4, 8, 16)
POOL_HALO = 8
SGU_GROUPS = 8
SGU_CHUNK = 128
HEAD_DIM = 128
V_DIM = 2 * HEAD_DIM
GRID_W = 64
ROPE_BASE = 10000.0
ROPE_HALF = HEAD_DIM // 4
LOG2E = 1.4426950408889634
KV_CHUNK = 512
Q_TILE = 1024

MXU_DTYPE = jnp.bfloat16
COND_ROWS = 16
V7X_VMEM_BYTES = 64 * 1024 * 1024
VMEM_CAP_BYTES = V7X_VMEM_BYTES - 8 * 1024 * 1024


def _vmem_limit(estimate_bytes):
    return int(min(max(estimate_bytes * 5 // 4, 16 * 1024 * 1024), VMEM_CAP_BYTES))


def _params(semantics, vmem_estimate):
    return pltpu.CompilerParams(dimension_semantics=semantics,
                                vmem_limit_bytes=_vmem_limit(vmem_estimate))


def _dot(a, b):
    return jnp.dot(a, b, preferred_element_type=jnp.float32)


def _norm_mod(x, g, shift, scale):
    ms = jnp.mean(x * x, axis=-1, keepdims=True)
    return (x * lax.rsqrt(ms + EPS)) * (g * (1.0 + scale)) + shift


def _emit_ffn_input(h2_ref, x_new, gf_ref, mod_ref):
    h2_ref[...] = _norm_mod(x_new, gf_ref[...], mod_ref[3:4, :], mod_ref[4:5, :]).astype(h2_ref.dtype)


def _split_pending(refs, pending, n=1):
    xs, rest = refs[:n], refs[n:]
    if not pending:
        return xs, None, None, rest
    return xs, rest[:n], rest[n], rest[n + 1:]


def _stream_rows(x_ref, f_ref, modp_ref):
    x = x_ref[...]
    if f_ref is not None:
        x = x + modp_ref[5:6, :] * f_ref[...]
    return x


def _largest_tile(n, cap, quantum):
    t = min(cap, n)
    t -= t % quantum
    while n % t:
        t -= quantum
    return t


def _ada_kernel(cond_ref, w_ref, b_ref, o_ref):
    c = cond_ref[...]
    s = (c * jax.nn.sigmoid(c)).astype(MXU_DTYPE)
    o_ref[...] = _dot(s, w_ref[...].astype(MXU_DTYPE)) + b_ref[...]


def _ada_mods(cond, ada_w, ada_b):
    depth, d, n = ada_w.shape
    bn = _largest_tile(n, 1024, 128)
    est = 2 * d * bn * 4 + d * bn * 2 + 4 * COND_ROWS * (d + bn) * 4
    return pl.pallas_call(
        _ada_kernel,
        grid=(depth, n // bn),
        in_specs=[
            pl.BlockSpec((COND_ROWS, d), lambda l, j: (0, 0)),
            pl.BlockSpec((None, d, bn), lambda l, j: (l, 0, j)),
            pl.BlockSpec((None, 1, bn), lambda l, j: (l, 0, j)),
        ],
        out_specs=pl.BlockSpec((None, COND_ROWS, bn), lambda l, j: (l, 0, j)),
        out_shape=jax.ShapeDtypeStruct((depth, COND_ROWS, n), jnp.float32),
        compiler_params=_params(("parallel", "parallel"), est),
        name="ada_mods",
    )(cond, ada_w, ada_b.reshape(depth, 1, n))


def _ffn_kernel(*refs, n_hidden_blocks, residual, final_norm, n_casts):
    refs = list(refs)
    if residual:
        x_ref, mod_ref = refs[:2]
        refs = refs[2:]
    h_ref, wa_ref, wb_ref, wo_ref = refs[:4]
    refs = refs[4:]
    if final_norm:
        fg_ref = refs.pop(0)
    cast_src, refs = refs[:n_casts], refs[n_casts:]
    o_ref, cast_dst = refs[0], refs[1:]
    j = pl.program_id(1)

    @pl.when(j == 0)
    def _():
        o_ref[...] = jnp.zeros_like(o_ref)

    for src, dst in zip(cast_src, cast_dst):
        dst[...] = src[...].astype(dst.dtype)

    h = h_ref[...]
    a = _dot(h, wa_ref[...])
    b = _dot(h, wb_ref[...])
    act = (a * jax.nn.sigmoid(a)) * b
    o_ref[...] += _dot(act.astype(MXU_DTYPE), wo_ref[...])

    if residual:
        @pl.when(j == n_hidden_blocks - 1)
        def _():
            o_ref[...] = x_ref[...] + mod_ref[5:6, :] * o_ref[...]
            if final_norm:
                y = o_ref[...]
                ms = jnp.mean(y * y, axis=-1, keepdims=True)
                o_ref[...] = (o_ref[...] * lax.rsqrt(ms + EPS)) * fg_ref[...]


def _cast_plan(shape, ni, nj):
    layers, r, c = shape
    rest = (layers - 1) * r

    def rows_per_block(max_blocks):
        return next((br for br in range(16, r + 1, 16) if r % br == 0 and rest // br <= max_blocks), None)

    if c % nj == 0 and (c // nj) % 128 == 0 and rows_per_block(ni):
        br = rows_per_block(ni)
        last, first = rest // br - 1, r // br
        return (pl.BlockSpec((br, c // nj), lambda i, j: (first + jnp.minimum(i, last), j)),
                pl.BlockSpec((br, c // nj), lambda i, j: (jnp.minimum(i, last), j)), (rest, c))
    br = rows_per_block(ni * nj)
    if br is None:
        return None
    last, first = rest // br - 1, r // br
    return (pl.BlockSpec((br, c), lambda i, j: (first + jnp.minimum(i * nj + j, last), 0)),
            pl.BlockSpec((br, c), lambda i, j: (jnp.minimum(i * nj + j, last), 0)), (rest, c))


def _ffn_grid(t, f, fused_residual=False):
    tm = _largest_tile(t, 512 if fused_residual else 1024, 16)
    th = _largest_tile(f, 512, 128)
    return tm, th, t // tm, f // th


def _ffn(h, w_in, w_out, layer, residual=None, final_g=None, casts=()):
    t, d = h.shape
    f = w_out.shape[1]
    fused = residual is not None
    tm, th, ni, nj = _ffn_grid(t, f, fused)
    row_spec = pl.BlockSpec((tm, d), lambda i, j: (i, 0))
    in_specs, args = [], []
    if fused:
        x, mods, mod_row = residual
        in_specs += [row_spec, pl.BlockSpec((None, N_MOD, d), lambda i, j: (mod_row(i, tm), 0, 0))]
        args += [x, mods]
    in_specs += [
        row_spec,
        pl.BlockSpec((None, d, th), lambda i, j: (layer, 0, j)),
        pl.BlockSpec((None, d, th), lambda i, j: (layer, 0, j + nj)),
        pl.BlockSpec((None, th, d), lambda i, j: (layer, j, 0)),
    ]
    args += [h, w_in, w_in, w_out]
    if final_g is not None:
        in_specs.append(pl.BlockSpec((1, d), lambda i, j: (0, 0)))
        args.append(final_g.reshape(1, d))
    plans = [_cast_plan(a.shape, ni, nj) for a in casts]
    in_specs += [p[0] for p in plans]
    args += [a.reshape(-1, a.shape[-1]) for a in casts]
    est = (4 if fused else 2) * tm * d * 4 + 2 * tm * d * 2 + 6 * d * th * 2 + 7 * tm * th * 4 \
        + sum(2 * 6 * math.prod(p[0].block_shape) for p in plans)
    out = pl.pallas_call(
        functools.partial(_ffn_kernel, n_hidden_blocks=nj, residual=fused, final_norm=final_g is not None,
                          n_casts=len(casts)),
        grid=(ni, nj),
        in_specs=in_specs,
        out_specs=[row_spec] + [p[1] for p in plans],
        out_shape=[jax.ShapeDtypeStruct((t, d), jnp.float32)]
        + [jax.ShapeDtypeStruct(p[2], MXU_DTYPE) for p in plans],
        compiler_params=_params(("parallel", "arbitrary"), est),
        name="ffn",
    )(*args)
    return out if casts else out[0]


def _pool_kernel(*refs, seq_len, blocks_per_seq, pending):
    (x_ref, xp_ref, xn_ref), fs, modp_ref, rest = _split_pending(list(refs), pending, 3)
    f_ref, fp_ref, fn_ref = fs if pending else (None, None, None)
    mod_ref, g_ref, gf_ref, w_ref, sc_ref, o_ref, h2_ref, h_sc, p_sc, d_sc = rest
    tm, d = x_ref.shape
    gd = d // len(POOL_WINDOWS)
    n = tm + 2 * POOL_HALO
    i = pl.program_id(0)
    blk = i % blocks_per_seq
    g = g_ref[...]
    shift = mod_ref[0:1, :]
    scale = mod_ref[1:2, :]
    x = _stream_rows(x_ref, f_ref, modp_ref)
    has_prev = (blk > 0).astype(jnp.float32)
    has_next = (blk < blocks_per_seq - 1).astype(jnp.float32)
    h_sc[0:POOL_HALO, :] = _norm_mod(_stream_rows(xp_ref, fp_ref, modp_ref), g, shift, scale) * has_prev
    h_sc[POOL_HALO:POOL_HALO + tm, :] = _norm_mod(x, g, shift, scale)
    h_sc[POOL_HALO + tm:n, :] = _norm_mod(_stream_rows(xn_ref, fn_ref, modp_ref), g, shift, scale) * has_next
    h_sc[n:, :] = jnp.zeros((POOL_HALO, d), jnp.float32)
    p_sc[n:, :] = jnp.zeros((POOL_HALO, gd), jnp.float32)

    pos = blk * tm + lax.broadcasted_iota(jnp.int32, (tm, 1), 0)
    for gi, win in enumerate(POOL_WINDOWS):
        cols = slice(gi * gd, (gi + 1) * gd)
        half = win // 2
        if half == 1:
            lo = h_sc[POOL_HALO - 1:POOL_HALO - 1 + tm, cols]
            hi = h_sc[POOL_HALO:POOL_HALO + tm, cols]
        else:
            p_sc[0:n, :] = h_sc[0:n, cols] + h_sc[1:n + 1, cols]
            s = 2
            while s < half:
                p_sc[0:n, :] = p_sc[0:n, :] + p_sc[s:n + s, :]
                s *= 2
            lo = p_sc[POOL_HALO - half:POOL_HALO - half + tm, :]
            hi = p_sc[POOL_HALO:POOL_HALO + tm, :]
        cnt = jnp.minimum(pos + half, seq_len) - jnp.maximum(pos - half, 0)
        inv_cnt = 1.0 / cnt.astype(jnp.float32)
        dev = (lo + hi) * inv_cnt - h_sc[POOL_HALO:POOL_HALO + tm, cols]
        d_sc[:, cols] = _dot(dev.astype(MXU_DTYPE), w_ref[gi])
    x_new = x + d_sc[...] * (mod_ref[2:3, :] * sc_ref[...])
    o_ref[...] = x_new
    _emit_ffn_input(h2_ref, x_new, gf_ref, mod_ref)


def _pool_mixer(x, pending, mods, mod_row, g, gf, w, scale, seq_len):
    t, d = x.shape
    tm = _largest_tile(seq_len, 512, 16)
    bps = seq_len // tm
    hb = tm // POOL_HALO
    n_halo_blocks = t // POOL_HALO
    gd = d // len(POOL_WINDOWS)
    row_spec = pl.BlockSpec((tm, d), lambda i: (i, 0))
    vec_spec = pl.BlockSpec((1, d), lambda i: (0, 0))
    mod_spec = pl.BlockSpec((None, N_MOD, d), lambda i: (mod_row(i, tm), 0, 0))
    halo_specs = [
        row_spec,
        pl.BlockSpec((POOL_HALO, d), lambda i: (jnp.maximum(i * hb - 1, 0), 0)),
        pl.BlockSpec((POOL_HALO, d), lambda i: (jnp.minimum((i + 1) * hb, n_halo_blocks - 1), 0)),
    ]
    in_specs, args = list(halo_specs), [x, x, x]
    if pending:
        f, mods_prev = pending
        in_specs += halo_specs + [mod_spec]
        args += [f, f, f, mods_prev]
    in_specs += [mod_spec, vec_spec, vec_spec, pl.BlockSpec(w.shape, lambda i: (0, 0, 0)), vec_spec]
    args += [mods, g.reshape(1, d), gf.reshape(1, d), w, scale.reshape(1, d)]
    est = (6 if pending else 4) * tm * d * 4 + 2 * tm * d * 2 + 2 * (tm + 3 * POOL_HALO) * d * 4 \
        + 2 * w.size * 2 + 6 * tm * gd * 4
    return pl.pallas_call(
        functools.partial(_pool_kernel, seq_len=seq_len, blocks_per_seq=bps, pending=bool(pending)),
        grid=(t // tm,),
        in_specs=in_specs,
        out_specs=[row_spec, row_spec],
        out_shape=[jax.ShapeDtypeStruct((t, d), jnp.float32), jax.ShapeDtypeStruct((t, d), MXU_DTYPE)],
        scratch_shapes=[pltpu.VMEM((tm + 3 * POOL_HALO, d), jnp.float32),
                        pltpu.VMEM((tm + 3 * POOL_HALO, gd), jnp.float32),
                        pltpu.VMEM((tm, d), jnp.float32)],
        compiler_params=_params(("parallel",), est),
        name="pool_mixer",
    )(*args)


def _sgu_kernel(*refs, pending):
    (x_ref,), fs, modp_ref, rest = _split_pending(list(refs), pending)
    mod_ref, g_ref, gf_ref, win_ref, ng_ref, ws_ref, bs_ref, wout_ref, o_ref, h2_ref, h_sc, v_sc, p_sc = rest
    tm, d = x_ref.shape
    sd = wout_ref.shape[0]
    gd = sd // SGU_GROUPS
    x = _stream_rows(x_ref, fs[0] if pending else None, modp_ref)
    h_sc[...] = _norm_mod(x, g_ref[...], mod_ref[0:1, :], mod_ref[1:2, :]).astype(MXU_DTYPE)
    h = h_sc[...]

    def gelu(z):
        return 0.5 * z * (1.0 + lax.erf(z * (1.0 / math.sqrt(2.0))))

    ssq = jnp.zeros((tm, 1), jnp.float32)
    for gi in range(SGU_GROUPS):
        v = gelu(_dot(h, win_ref[:, sd + gi * gd:sd + (gi + 1) * gd]))
        ssq = ssq + jnp.sum(v * v, axis=-1, keepdims=True)
        v_sc[:, gi * gd:(gi + 1) * gd] = v
    rstd = lax.rsqrt(ssq * (1.0 / sd) + EPS)

    for gi in range(SGU_GROUPS):
        cols = slice(gi * gd, (gi + 1) * gd)
        vn = (v_sc[:, cols] * rstd * ng_ref[:, cols]).astype(MXU_DTYPE)
        u = gelu(_dot(h, win_ref[:, cols]))
        wsg = ws_ref[gi]
        bias = bs_ref[gi]
        for c in range(tm // SGU_CHUNK):
            rows = slice(c * SGU_CHUNK, (c + 1) * SGU_CHUNK)
            vm = _dot(wsg, vn[rows, :]) + bias
            p_sc[rows, cols] = (u[rows, :] * vm).astype(MXU_DTYPE)
    x_new = x + mod_ref[2:3, :] * _dot(p_sc[...], wout_ref[...])
    o_ref[...] = x_new
    _emit_ffn_input(h2_ref, x_new, gf_ref, mod_ref)


def _sgu_mixer(x, pending, mods, mod_row, g, gf, w_in, norm_g, ws, bs, w_out):
    t, d = x.shape
    sd = w_out.shape[0]
    tm = _largest_tile(t, 256, SGU_CHUNK)
    est = (w_in.size + w_out.size + ws.size) * 2 + (6 if pending else 4) * tm * d * 4 + tm * sd * 4 \
        + 4 * tm * d * 2 + 6 * tm * (sd // SGU_GROUPS) * 4
    const = dict(pipeline_mode=pl.Buffered(1))
    row_spec = pl.BlockSpec((tm, d), lambda i: (i, 0))
    vec_spec = pl.BlockSpec((1, d), lambda i: (0, 0))
    mod_spec = pl.BlockSpec((None, N_MOD, d), lambda i: (mod_row(i, tm), 0, 0))
    in_specs, args = [row_spec], [x]
    if pending:
        in_specs += [row_spec, mod_spec]
        args += list(pending)
    in_specs += [
        mod_spec,
        vec_spec,
        vec_spec,
        pl.BlockSpec(w_in.shape, lambda i: (0, 0), **const),
        pl.BlockSpec((1, sd), lambda i: (0, 0)),
        pl.BlockSpec(ws.shape, lambda i: (0, 0, 0), **const),
        pl.BlockSpec(bs.shape, lambda i: (0, 0, 0), **const),
        pl.BlockSpec(w_out.shape, lambda i: (0, 0), **const),
    ]
    args += [mods, g.reshape(1, d), gf.reshape(1, d), w_in, norm_g.reshape(1, sd), ws, bs, w_out]
    return pl.pallas_call(
        functools.partial(_sgu_kernel, pending=bool(pending)),
        grid=(t // tm,),
        in_specs=in_specs,
        out_specs=[row_spec, row_spec],
        out_shape=[jax.ShapeDtypeStruct((t, d), jnp.float32), jax.ShapeDtypeStruct((t, d), MXU_DTYPE)],
        scratch_shapes=[pltpu.VMEM((tm, d), MXU_DTYPE), pltpu.VMEM((tm, sd), jnp.float32),
                        pltpu.VMEM((tm, sd), MXU_DTYPE)],
        compiler_params=_params(("parallel",), est),
        name="sgu_mixer",
    )(*args)


def _rope_tables(n_tokens):
    rows = n_tokens // GRID_W
    row = jnp.broadcast_to(jnp.arange(rows, dtype=jnp.float32)[:, None], (rows, GRID_W)).reshape(-1)
    col = jnp.broadcast_to(jnp.arange(GRID_W, dtype=jnp.float32)[None, :], (rows, GRID_W)).reshape(-1)
    inv = ROPE_BASE ** (-jnp.arange(ROPE_HALF, dtype=jnp.float32) / ROPE_HALF)
    ang = jnp.stack([row[:, None] * inv, col[:, None] * inv], axis=1)
    cos, sin = jnp.cos(ang), jnp.sin(ang)
    c = jnp.stack([cos, cos], axis=2).reshape(n_tokens, HEAD_DIM)
    s = jnp.stack([-sin, sin], axis=2).reshape(n_tokens, HEAD_DIM)
    return c, s


def _qkv_kernel(*refs, pending, rope, keep_f32, q_scale):
    (x_ref,), fs, modp_ref, rest = _split_pending(list(refs), pending)
    mod_ref, g_ref, w_ref = rest[:3]
    rest = rest[3:]
    if rope:
        c_ref, s_ref = rest[:2]
        rest = rest[2:]
    q_ref, k_ref, v_ref = rest[:3]
    rest = rest[3:]
    if keep_f32:
        kf_ref, vf_ref = rest[:2]
        rest = rest[2:]
    (h_sc,) = rest
    tm, d = x_ref.shape
    j = pl.program_id(1)

    @pl.when(j == 0)
    def _():
        x = _stream_rows(x_ref, fs[0] if pending else None, modp_ref)
        h = _norm_mod(x, g_ref[...], mod_ref[0:1, :], mod_ref[1:2, :])
        h_sc[...] = h.astype(MXU_DTYPE)

    def rotate(dst_ref, y, mult):
        c = c_ref[...]
        s = s_ref[...]
        lane = lax.broadcasted_iota(jnp.int32, (tm, HEAD_DIM), 1)
        low_half = (lane % (2 * ROPE_HALF)) < ROPE_HALF
        for gi in range(d // HEAD_DIM):
            cols = slice(gi * HEAD_DIM, (gi + 1) * HEAD_DIM)
            yg = y[:, cols]
            partner = jnp.where(low_half,
                                pltpu.roll(yg, HEAD_DIM - ROPE_HALF, 1),
                                pltpu.roll(yg, ROPE_HALF, 1))
            r = yg * c + partner * s
            if mult != 1.0:
                r = r * mult
            dst_ref[:, cols] = r.astype(dst_ref.dtype)

    @pl.when(j == 0)
    def _():
        y = _dot(h_sc[...], w_ref[...])
        if rope:
            rotate(q_ref, y, q_scale)
        else:
            q_ref[...] = (y * q_scale).astype(q_ref.dtype)

    @pl.when(j == 1)
    def _():
        y = _dot(h_sc[...], w_ref[...])
        if keep_f32:
            kf_ref[...] = y
        if rope:
            rotate(k_ref, y, 1.0)
        else:
            k_ref[...] = y.astype(k_ref.dtype)

    @pl.when(j == 2)
    def _():
        y = _dot(h_sc[...], w_ref[...])
        if keep_f32:
            vf_ref[...] = y
        v_ref[...] = y.astype(v_ref.dtype)


def _qkv(x, pending, mods, mod_row, g, w_qkv, seq_len, rope, keep_f32):
    t, d = x.shape
    tm = _largest_tile(seq_len, 256 if keep_f32 else 512, 16)
    bps = seq_len // tm
    q_scale = HEAD_DIM ** -0.5 * LOG2E
    row_spec = pl.BlockSpec((tm, d), lambda i, j: (i, 0))
    mod_spec = pl.BlockSpec((None, N_MOD, d), lambda i, j: (mod_row(i, tm), 0, 0))
    in_specs, args = [row_spec], [x]
    if pending:
        in_specs += [row_spec, mod_spec]
        args += list(pending)
    in_specs += [mod_spec, pl.BlockSpec((1, d), lambda i, j: (0, 0)), pl.BlockSpec((d, d), lambda i, j: (0, j))]
    args += [mods, g.reshape(1, d), w_qkv]
    if rope:
        c, s = _rope_tables(seq_len)
        in_specs += [pl.BlockSpec((tm, HEAD_DIM), lambda i, j: (i % bps, 0))] * 2
        args += [c, s]
    out_specs = [row_spec] * 3
    out_shape = [jax.ShapeDtypeStruct((t, d), MXU_DTYPE)] * 3
    if keep_f32:
        out_specs += [row_spec] * 2
        out_shape += [jax.ShapeDtypeStruct((t, d), jnp.float32)] * 2
    est = (4 if pending else 2) * tm * d * 4 + 2 * d * d * 2 + tm * d * 2 + 6 * tm * d * 2 + 3 * tm * d * 4 \
        + (4 * tm * d * 4 if keep_f32 else 0)
    return pl.pallas_call(
        functools.partial(_qkv_kernel, pending=bool(pending), rope=rope, keep_f32=keep_f32, q_scale=q_scale),
        grid=(t // tm, 3),
        in_specs=in_specs,
        out_specs=out_specs,
        out_shape=out_shape,
        scratch_shapes=[pltpu.VMEM((tm, d), MXU_DTYPE)],
        compiler_params=_params(("parallel", "arbitrary"), est),
        name="attn_qkv",
    )(*args)


def _attn_kernel(q_ref, k_ref, v_ref, *rest, lam_init, has_cache):
    if has_cache:
        ck_ref, cv_ref, lp_ref, sg_ref, o_ref = rest
        cached_v = cv_ref[...].astype(MXU_DTYPE)
    else:
        lp_ref, sg_ref, o_ref = rest
    n_new = k_ref.shape[0]
    chunk = min(KV_CHUNK, n_new)
    lp = lp_ref[...]
    lam = (jnp.exp(jnp.sum(lp[0:1, :] * lp[1:2, :], axis=-1, keepdims=True))
           - jnp.exp(jnp.sum(lp[2:3, :] * lp[3:4, :], axis=-1, keepdims=True)) + lam_init)
    outs = []
    for mp in range(2):
        cols = slice(mp * HEAD_DIM, (mp + 1) * HEAD_DIM)
        q = q_ref[:, cols]
        kv = []
        if has_cache:
            kv.append((ck_ref[:, cols].astype(MXU_DTYPE), cached_v))
        for c0 in range(0, n_new, chunk):
            kv.append((k_ref[c0:c0 + chunk, cols], v_ref[c0:c0 + chunk, :]))
        m = denom = acc = None
        for kc, vc in kv:
            s = lax.dot_general(q, kc, (((1,), (1,)), ((), ())), preferred_element_type=jnp.float32)
            mc = jnp.max(s, axis=-1, keepdims=True)
            if m is None:
                m = mc
                p = jnp.exp2(s - m)
                denom = jnp.sum(p, axis=-1, keepdims=True)
                acc = _dot(p.astype(MXU_DTYPE), vc)
            else:
                m_new = jnp.maximum(m, mc)
                alpha = jnp.exp2(m - m_new)
                p = jnp.exp2(s - m_new)
                denom = alpha * denom + jnp.sum(p, axis=-1, keepdims=True)
                acc = alpha * acc + _dot(p.astype(MXU_DTYPE), vc)
                m = m_new
        outs.append(acc / denom)
    o = outs[0] - lam * outs[1]
    ms = jnp.mean(o * o, axis=-1, keepdims=True)
    o = (o * lax.rsqrt(ms + EPS) * sg_ref[...]) * (1.0 - lam_init)
    o_ref[...] = o.astype(o_ref.dtype)


def _attention(q, k, v, cache, lam_params, subln_g, lam_init):
    b, n, d = q.shape
    heads = d // V_DIM
    tq = _largest_tile(n, Q_TILE, 16)
    in_specs = [
        pl.BlockSpec((None, tq, V_DIM), lambda bi, hi, qi: (bi, qi, hi)),
        pl.BlockSpec((None, n, V_DIM), lambda bi, hi, qi: (bi, 0, hi)),
        pl.BlockSpec((None, n, V_DIM), lambda bi, hi, qi: (bi, 0, hi)),
    ]
    args = [q, k, v]
    past = 0
    if cache is not None:
        cache_k, cache_v, slot = cache
        past = cache_k.shape[2]
        in_specs += [pl.BlockSpec((None, None, past, V_DIM), lambda bi, hi, qi: (bi, slot, 0, hi))] * 2
        args += [cache_k, cache_v]
    in_specs += [
        pl.BlockSpec(lam_params.shape, lambda bi, hi, qi: (0, 0)),
        pl.BlockSpec((1, V_DIM), lambda bi, hi, qi: (0, 0)),
    ]
    args += [lam_params, subln_g.reshape(1, V_DIM)]
    chunk = min(KV_CHUNK, n)
    est = 4 * tq * V_DIM * 2 + 4 * n * V_DIM * 2 + 4 * past * V_DIM * 4 \
        + 6 * tq * chunk * 4 + 8 * tq * V_DIM * 4
    return pl.pallas_call(
        functools.partial(_attn_kernel, lam_init=lam_init, has_cache=cache is not None),
        grid=(b, heads, n // tq),
        in_specs=in_specs,
        out_specs=pl.BlockSpec((None, tq, V_DIM), lambda bi, hi, qi: (bi, qi, hi)),
        out_shape=jax.ShapeDtypeStruct((b, n, d), MXU_DTYPE),
        compiler_params=_params(("parallel", "parallel", "parallel"), est),
        name="diff_attention",
    )(*args)


def _proj_kernel(*refs, pending):
    (x_ref,), fs, modp_ref, rest = _split_pending(list(refs), pending)
    a_ref, w_ref, mod_ref, gf_ref, o_ref, h2_ref = rest
    x = _stream_rows(x_ref, fs[0] if pending else None, modp_ref)
    x_new = x + mod_ref[2:3, :] * _dot(a_ref[...], w_ref[...])
    o_ref[...] = x_new
    _emit_ffn_input(h2_ref, x_new, gf_ref, mod_ref)


def _attn_out_proj(x, pending, a, w_o, mods, mod_row, gf):
    t, d = x.shape
    tm = _largest_tile(t, 512, 16)
    est = 4 * tm * d * 2 + 2 * d * d * 2 + (8 if pending else 6) * tm * d * 4
    row_spec = pl.BlockSpec((tm, d), lambda i: (i, 0))
    mod_spec = pl.BlockSpec((None, N_MOD, d), lambda i: (mod_row(i, tm), 0, 0))
    in_specs, args = [row_spec], [x]
    if pending:
        in_specs += [row_spec, mod_spec]
        args += list(pending)
    in_specs += [row_spec, pl.BlockSpec((d, d), lambda i: (0, 0)), mod_spec, pl.BlockSpec((1, d), lambda i: (0, 0))]
    args += [a, w_o, mods, gf.reshape(1, d)]
    return pl.pallas_call(
        functools.partial(_proj_kernel, pending=bool(pending)),
        grid=(t // tm,),
        in_specs=in_specs,
        out_specs=[row_spec, row_spec],
        out_shape=[jax.ShapeDtypeStruct((t, d), jnp.float32), jax.ShapeDtypeStruct((t, d), MXU_DTYPE)],
        compiler_params=_params(("parallel",), est),
        name="attn_out_proj",
    )(*args)


def kernel(x_prompt, x_sample, cache_k, cache_v, c, c_ctx, ada_w, ada_b, norm_mix_g, norm_ffn_g,
           pool_w, pool_scale, sgu_w_in, sgu_norm_g, sgu_ws, sgu_b, sgu_w_out,
           attn_w_qkv, attn_lambda, attn_subln_g, attn_w_o, ffn_w_in, ffn_w_out, final_g):
    batch, seq, d = x_prompt.shape
    dec_batch, dec_seq, _ = x_sample.shape
    depth = ada_w.shape[0]
    assert dec_batch < COND_ROWS and d % V_DIM == 0 and dec_seq % GRID_W == 0

    cond = jnp.zeros((COND_ROWS, d), jnp.float32).at[:dec_batch].set(c).at[dec_batch].set(c_ctx)
    mods = _ada_mods(cond, ada_w, ada_b).reshape(depth, COND_ROWS, N_MOD, d)

    cast = lambda w: w.astype(MXU_DTYPE)
    hidden = ffn_w_out.shape[1]
    _, _, ni, nj = _ffn_grid(dec_batch * dec_seq, hidden)
    side_cast = depth > 1 and all(_cast_plan(w.shape, ni, nj) for w in (ffn_w_in, ffn_w_out))
    if side_cast:
        ffn_w = [(cast(ffn_w_in[:1]), cast(ffn_w_out[:1]), 0)]
    else:
        w_in_all, w_out_all = cast(ffn_w_in), cast(ffn_w_out)
        ffn_w = [(w_in_all, w_out_all, l) for l in range(depth)]
    cache_k = cache_k.reshape(cache_k.shape[:3] + (d,))
    cache_v = cache_v.reshape(cache_v.shape[:3] + (d,))
    streams = {
        "ctx": (x_prompt.reshape(batch * seq, d), seq, lambda i, tm: dec_batch),
        "lat": (x_sample.reshape(dec_batch * dec_seq, d), dec_seq,
                lambda i, tm: (i * tm) // dec_seq),
    }
    xs = {name: s[0] for name, s in streams.items()}
    pend = {name: None for name in streams}
    new_k, new_v = [], []
    for i in range(depth):
        kind, slot = i % N_MIXERS, i // N_MIXERS
        m_i = mods[i]
        gf = norm_ffn_g[i]
        last = i == depth - 1
        for name, (_, n, mod_row) in streams.items():
            x, pending = xs[name], pend[name]
            if kind == 0:
                x, h2 = _pool_mixer(x, pending, m_i, mod_row, norm_mix_g[i], gf, cast(pool_w[slot]),
                                    pool_scale[slot], n)
            elif kind == 1:
                x, h2 = _sgu_mixer(x, pending, m_i, mod_row, norm_mix_g[i], gf, cast(sgu_w_in[slot]),
                                   sgu_norm_g[slot], cast(sgu_ws[slot]), sgu_b[slot][:, :, None],
                                   cast(sgu_w_out[slot]))
            else:
                lam_init = 0.8 - 0.6 * math.exp(-0.3 * i)
                is_ctx = name == "ctx"
                outs = _qkv(x, pending, m_i, mod_row, norm_mix_g[i], cast(attn_w_qkv[slot]), n,
                            rope=not is_ctx, keep_f32=is_ctx)
                nb = x.shape[0] // n
                q, k, v = (a.reshape(nb, n, d) for a in outs[:3])
                if is_ctx:
                    new_k.append(outs[3])
                    new_v.append(outs[4])
                a = _attention(q, k, v, None if is_ctx else (cache_k, cache_v, slot),
                               attn_lambda[slot], attn_subln_g[slot], lam_init)
                x, h2 = _attn_out_proj(x, pending, a.reshape(-1, d), cast(attn_w_o[slot]), m_i, mod_row, gf)
            if last:
                xs[name] = _ffn(h2, *ffn_w[i], residual=(x, m_i, mod_row), final_g=final_g)
                pend[name] = None
            elif side_cast and i == 0 and name == "lat":
                f, w_in_rest, w_out_rest = _ffn(h2, *ffn_w[0], casts=(ffn_w_in, ffn_w_out))
                ffn_w += [(w_in_rest.reshape((depth - 1,) + ffn_w_in.shape[1:]),
                           w_out_rest.reshape((depth - 1,) + ffn_w_out.shape[1:]), l - 1)
                          for l in range(1, depth)]
                xs[name], pend[name] = x, (f, m_i)
            else:
                xs[name], pend[name] = x, (_ffn(h2, *ffn_w[i]), m_i)
    heads = d // V_DIM
    y_prompt = xs["ctx"].reshape(batch, seq, d)
    y_sample = xs["lat"].reshape(dec_batch, dec_seq, d)
    new_cache_k = jnp.stack([a.reshape(batch, seq, heads, 2, HEAD_DIM) for a in new_k], axis=1)
    new_cache_v = jnp.stack([a.reshape(batch, seq, heads, V_DIM) for a in new_v], axis=1)
    return (y_prompt, y_sample, new_cache_k, new_cache_v)
```

```python
import functools
import math

import jax
import jax.numpy as jnp
from jax import lax
from jax.experimental import pallas as pl
from jax.experimental.pallas import tpu as pltpu

EPS = 1e-6
N_MOD = 6
N_MIXERS = 3
POOL_WINDOWS = (2, 4, 8, 16)
POOL_HALO = 8
SGU_GROUPS = 8
SGU_CHUNK = 128
HEAD_DIM = 128
V_DIM = 2 * HEAD_DIM
GRID_W = 64
ROPE_BASE = 10000.0
ROPE_HALF = HEAD_DIM // 4
LOG2E = 1.4426950408889634
KV_CHUNK = 512
Q_TILE = 1024

MXU_DTYPE = jnp.bfloat16
COND_ROWS = 16
V7X_VMEM_BYTES = 64 * 1024 * 1024
VMEM_CAP_BYTES = V7X_VMEM_BYTES - 8 * 1024 * 1024


def _vmem_limit(estimate_bytes):
    return int(min(max(estimate_bytes * 5 // 4, 16 * 1024 * 1024), VMEM_CAP_BYTES))


def _params(semantics, vmem_estimate):
    return pltpu.CompilerParams(dimension_semantics=semantics,
                                vmem_limit_bytes=_vmem_limit(vmem_estimate))


def _dot(a, b):
    return jnp.dot(a, b, preferred_element_type=jnp.float32)


def _norm_mod(x, g, shift, scale):
    ms = jnp.mean(x * x, axis=-1, keepdims=True)
    return (x * lax.rsqrt(ms + EPS)) * (g * (1.0 + scale)) + shift


def _emit_ffn_input(h2_ref, x_new, gf_ref, mod_ref):
    h2_ref[...] = _norm_mod(x_new, gf_ref[...], mod_ref[3:4, :], mod_ref[4:5, :]).astype(h2_ref.dtype)


def _split_pending(refs, pending, n=1):
    xs, rest = refs[:n], refs[n:]
    if not pending:
        return xs, None, None, rest
    return xs, rest[:n], rest[n], rest[n + 1:]


def _stream_rows(x_ref, f_ref, modp_ref):
    x = x_ref[...]
    if f_ref is not None:
        x = x + modp_ref[5:6, :] * f_ref[...]
    return x


def _largest_tile(n, cap, quantum):
    t = min(cap, n)
    t -= t % quantum
    while n % t:
        t -= quantum
    return t


def _ada_kernel(cond_ref, w_ref, b_ref, o_ref):
    c = cond_ref[...]
    s = (c * jax.nn.sigmoid(c)).astype(MXU_DTYPE)
    o_ref[...] = _dot(s, w_ref[...].astype(MXU_DTYPE)) + b_ref[...]


def _ada_mods(cond, ada_w, ada_b):
    depth, d, n = ada_w.shape
    bn = _largest_tile(n, 1024, 128)
    est = 2 * d * bn * 4 + d * bn * 2 + 4 * COND_ROWS * (d + bn) * 4
    return pl.pallas_call(
        _ada_kernel,
        grid=(depth, n // bn),
        in_specs=[
            pl.BlockSpec((COND_ROWS, d), lambda l, j: (0, 0)),
            pl.BlockSpec((None, d, bn), lambda l, j: (l, 0, j)),
            pl.BlockSpec((None, 1, bn), lambda l, j: (l, 0, j)),
        ],
        out_specs=pl.BlockSpec((None, COND_ROWS, bn), lambda l, j: (l, 0, j)),
        out_shape=jax.ShapeDtypeStruct((depth, COND_ROWS, n), jnp.float32),
        compiler_params=_params(("parallel", "parallel"), est),
        name="ada_mods",
    )(cond, ada_w, ada_b.reshape(depth, 1, n))


def _ffn_kernel(*refs, n_hidden_blocks, residual, final_norm, n_casts):
    refs = list(refs)
    if residual:
        x_ref, mod_ref = refs[:2]
        refs = refs[2:]
    h_ref, wa_ref, wb_ref, wo_ref = refs[:4]
    refs = refs[4:]
    if final_norm:
        fg_ref = refs.pop(0)
    cast_src, refs = refs[:n_casts], refs[n_casts:]
    o_ref, cast_dst = refs[0], refs[1:]
    j = pl.program_id(1)

    @pl.when(j == 0)
    def _():
        o_ref[...] = jnp.zeros_like(o_ref)

    for src, dst in zip(cast_src, cast_dst):
        dst[...] = src[...].astype(dst.dtype)

    h = h_ref[...]
    a = _dot(h, wa_ref[...])
    b = _dot(h, wb_ref[...])
    act = (a * jax.nn.sigmoid(a)) * b
    o_ref[...] += _dot(act.astype(MXU_DTYPE), wo_ref[...])

    if residual:
        @pl.when(j == n_hidden_blocks - 1)
        def _():
            o_ref[...] = x_ref[...] + mod_ref[5:6, :] * o_ref[...]
            if final_norm:
                y = o_ref[...]
                ms = jnp.mean(y * y, axis=-1, keepdims=True)
                o_ref[...] = (o_ref[...] * lax.rsqrt(ms + EPS)) * fg_ref[...]


def _cast_plan(shape, ni, nj):
    layers, r, c = shape
    rest = (layers - 1) * r

    def rows_per_block(max_blocks):
        return next((br for br in range(16, r + 1, 16) if r % br == 0 and rest // br <= max_blocks), None)

    if c % nj == 0 and (c // nj) % 128 == 0 and rows_per_block(ni):
        br = rows_per_block(ni)
        last, first = rest // br - 1, r // br
        return (pl.BlockSpec((br, c // nj), lambda i, j: (first + jnp.minimum(i, last), j)),
                pl.BlockSpec((br, c // nj), lambda i, j: (jnp.minimum(i, last), j)), (rest, c))
    br = rows_per_block(ni * nj)
    if br is None:
        return None
    last, first = rest // br - 1, r // br
    return (pl.BlockSpec((br, c), lambda i, j: (first + jnp.minimum(i * nj + j, last), 0)),
            pl.BlockSpec((br, c), lambda i, j: (jnp.minimum(i * nj + j, last), 0)), (rest, c))


def _ffn_grid(t, f, fused_residual=False):
    tm = _largest_tile(t, 512 if fused_residual else 1024, 16)
    th = _largest_tile(f, 512, 128)
    return tm, th, t // tm, f // th


def _ffn(h, w_in, w_out, layer, residual=None, final_g=None, casts=()):
    t, d = h.shape
    f = w_out.shape[1]
    fused = residual is not None
    tm, th, ni, nj = _ffn_grid(t, f, fused)
    row_spec = pl.BlockSpec((tm, d), lambda i, j: (i, 0))
    in_specs, args = [], []
    if fused:
        x, mods, mod_row = residual
        in_specs += [row_spec, pl.BlockSpec((None, N_MOD, d), lambda i, j: (mod_row(i, tm), 0, 0))]
        args += [x, mods]
    in_specs += [
        row_spec,
        pl.BlockSpec((None, d, th), lambda i, j: (layer, 0, j)),
        pl.BlockSpec((None, d, th), lambda i, j: (layer, 0, j + nj)),
        pl.BlockSpec((None, th, d), lambda i, j: (layer, j, 0)),
    ]
    args += [h, w_in, w_in, w_out]
    if final_g is not None:
        in_specs.append(pl.BlockSpec((1, d), lambda i, j: (0, 0)))
        args.append(final_g.reshape(1, d))
    plans = [_cast_plan(a.shape, ni, nj) for a in casts]
    in_specs += [p[0] for p in plans]
    args += [a.reshape(-1, a.shape[-1]) for a in casts]
    est = (4 if fused else 2) * tm * d * 4 + 2 * tm * d * 2 + 6 * d * th * 2 + 7 * tm * th * 4 \
        + sum(2 * 6 * math.prod(p[0].block_shape) for p in plans)
    out = pl.pallas_call(
        functools.partial(_ffn_kernel, n_hidden_blocks=nj, residual=fused, final_norm=final_g is not None,
                          n_casts=len(casts)),
        grid=(ni, nj),
        in_specs=in_specs,
        out_specs=[row_spec] + [p[1] for p in plans],
        out_shape=[jax.ShapeDtypeStruct((t, d), jnp.float32)]
        + [jax.ShapeDtypeStruct(p[2], MXU_DTYPE) for p in plans],
        compiler_params=_params(("parallel", "arbitrary"), est),
        name="ffn",
    )(*args)
    return out if casts else out[0]


def _pool_kernel(*refs, seq_len, blocks_per_seq, pending):
    (x_ref, xp_ref, xn_ref), fs, modp_ref, rest = _split_pending(list(refs), pending, 3)
    f_ref, fp_ref, fn_ref = fs if pending else (None, None, None)
    mod_ref, g_ref, gf_ref, w_ref, sc_ref, o_ref, h2_ref, h_sc, p_sc, d_sc = rest
    tm, d = x_ref.shape
    gd = d // len(POOL_WINDOWS)
    n = tm + 2 * POOL_HALO
    i = pl.program_id(0)
    blk = i % blocks_per_seq
    g = g_ref[...]
    shift = mod_ref[0:1, :]
    scale = mod_ref[1:2, :]
    x = _stream_rows(x_ref, f_ref, modp_ref)
    has_prev = (blk > 0).astype(jnp.float32)
    has_next = (blk < blocks_per_seq - 1).astype(jnp.float32)
    h_sc[0:POOL_HALO, :] = _norm_mod(_stream_rows(xp_ref, fp_ref, modp_ref), g, shift, scale) * has_prev
    h_sc[POOL_HALO:POOL_HALO + tm, :] = _norm_mod(x, g, shift, scale)
    h_sc[POOL_HALO + tm:n, :] = _norm_mod(_stream_rows(xn_ref, fn_ref, modp_ref), g, shift, scale) * has_next
    h_sc[n:, :] = jnp.zeros((POOL_HALO, d), jnp.float32)
    p_sc[n:, :] = jnp.zeros((POOL_HALO, gd), jnp.float32)

    pos = blk * tm + lax.broadcasted_iota(jnp.int32, (tm, 1), 0)
    for gi, win in enumerate(POOL_WINDOWS):
        cols = slice(gi * gd, (gi + 1) * gd)
        half = win // 2
        if half == 1:
            lo = h_sc[POOL_HALO - 1:POOL_HALO - 1 + tm, cols]
            hi = h_sc[POOL_HALO:POOL_HALO + tm, cols]
        else:
            p_sc[0:n, :] = h_sc[0:n, cols] + h_sc[1:n + 1, cols]
            s = 2
            while s < half:
                p_sc[0:n, :] = p_sc[0:n, :] + p_sc[s:n + s, :]
                s *= 2
            lo = p_sc[POOL_HALO - half:POOL_HALO - half + tm, :]
            hi = p_sc[POOL_HALO:POOL_HALO + tm, :]
        cnt = jnp.minimum(pos + half, seq_len) - jnp.maximum(pos - half, 0)
        inv_cnt = 1.0 / cnt.astype(jnp.float32)
        dev = (lo + hi) * inv_cnt - h_sc[POOL_HALO:POOL_HALO + tm, cols]
        d_sc[:, cols] = _dot(dev.astype(MXU_DTYPE), w_ref[gi])
    x_new = x + d_sc[...] * (mod_ref[2:3, :] * sc_ref[...])
    o_ref[...] = x_new
    _emit_ffn_input(h2_ref, x_new, gf_ref, mod_ref)


def _pool_mixer(x, pending, mods, mod_row, g, gf, w, scale, seq_len):
    t, d = x.shape
    tm = _largest_tile(seq_len, 512, 16)
    bps = seq_len // tm
    hb = tm // POOL_HALO
    n_halo_blocks = t // POOL_HALO
    gd = d // len(POOL_WINDOWS)
    row_spec = pl.BlockSpec((tm, d), lambda i: (i, 0))
    vec_spec = pl.BlockSpec((1, d), lambda i: (0, 0))
    mod_spec = pl.BlockSpec((None, N_MOD, d), lambda i: (mod_row(i, tm), 0, 0))
    halo_specs = [
        row_spec,
        pl.BlockSpec((POOL_HALO, d), lambda i: (jnp.maximum(i * hb - 1, 0), 0)),
        pl.BlockSpec((POOL_HALO, d), lambda i: (jnp.minimum((i + 1) * hb, n_halo_blocks - 1), 0)),
    ]
    in_specs, args = list(halo_specs), [x, x, x]
    if pending:
        f, mods_prev = pending
        in_specs += halo_specs + [mod_spec]
        args += [f, f, f, mods_prev]
    in_specs += [mod_spec, vec_spec, vec_spec, pl.BlockSpec(w.shape, lambda i: (0, 0, 0)), vec_spec]
    args += [mods, g.reshape(1, d), gf.reshape(1, d), w, scale.reshape(1, d)]
    est = (6 if pending else 4) * tm * d * 4 + 2 * tm * d * 2 + 2 * (tm + 3 * POOL_HALO) * d * 4 \
        + 2 * w.size * 2 + 6 * tm * gd * 4
    return pl.pallas_call(
        functools.partial(_pool_kernel, seq_len=seq_len, blocks_per_seq=bps, pending=bool(pending)),
        grid=(t // tm,),
        in_specs=in_specs,
        out_specs=[row_spec, row_spec],
        out_shape=[jax.ShapeDtypeStruct((t, d), jnp.float32), jax.ShapeDtypeStruct((t, d), MXU_DTYPE)],
        scratch_shapes=[pltpu.VMEM((tm + 3 * POOL_HALO, d), jnp.float32),
                        pltpu.VMEM((tm + 3 * POOL_HALO, gd), jnp.float32),
                        pltpu.VMEM((tm, d), jnp.float32)],
        compiler_params=_params(("parallel",), est),
        name="pool_mixer",
    )(*args)


def _sgu_kernel(*refs, pending):
    (x_ref,), fs, modp_ref, rest = _split_pending(list(refs), pending)
    mod_ref, g_ref, gf_ref, win_ref, ng_ref, ws_ref, bs_ref, wout_ref, o_ref, h2_ref, h_sc, v_sc, p_sc = rest
    tm, d = x_ref.shape
    sd = wout_ref.shape[0]
    gd = sd // SGU_GROUPS
    x = _stream_rows(x_ref, fs[0] if pending else None, modp_ref)
    h_sc[...] = _norm_mod(x, g_ref[...], mod_ref[0:1, :], mod_ref[1:2, :]).astype(MXU_DTYPE)
    h = h_sc[...]

    def gelu(z):
        return 0.5 * z * (1.0 + lax.erf(z * (1.0 / math.sqrt(2.0))))

    ssq = jnp.zeros((tm, 1), jnp.float32)
    for gi in range(SGU_GROUPS):
        v = gelu(_dot(h, win_ref[:, sd + gi * gd:sd + (gi + 1) * gd]))
        ssq = ssq + jnp.sum(v * v, axis=-1, keepdims=True)
        v_sc[:, gi * gd:(gi + 1) * gd] = v
    rstd = lax.rsqrt(ssq * (1.0 / sd) + EPS)

    for gi in range(SGU_GROUPS):
        cols = slice(gi * gd, (gi + 1) * gd)
        vn = (v_sc[:, cols] * rstd * ng_ref[:, cols]).astype(MXU_DTYPE)
        u = gelu(_dot(h, win_ref[:, cols]))
        wsg = ws_ref[gi]
        bias = bs_ref[gi]
        for c in range(tm // SGU_CHUNK):
            rows = slice(c * SGU_CHUNK, (c + 1) * SGU_CHUNK)
            vm = _dot(wsg, vn[rows, :]) + bias
            p_sc[rows, cols] = (u[rows, :] * vm).astype(MXU_DTYPE)
    x_new = x + mod_ref[2:3, :] * _dot(p_sc[...], wout_ref[...])
    o_ref[...] = x_new
    _emit_ffn_input(h2_ref, x_new, gf_ref, mod_ref)


def _sgu_mixer(x, pending, mods, mod_row, g, gf, w_in, norm_g, ws, bs, w_out):
    t, d = x.shape
    sd = w_out.shape[0]
    tm = _largest_tile(t, 256, SGU_CHUNK)
    est = (w_in.size + w_out.size + ws.size) * 2 + (6 if pending else 4) * tm * d * 4 + tm * sd * 4 \
        + 4 * tm * d * 2 + 6 * tm * (sd // SGU_GROUPS) * 4
    const = dict(pipeline_mode=pl.Buffered(1))
    row_spec = pl.BlockSpec((tm, d), lambda i: (i, 0))
    vec_spec = pl.BlockSpec((1, d), lambda i: (0, 0))
    mod_spec = pl.BlockSpec((None, N_MOD, d), lambda i: (mod_row(i, tm), 0, 0))
    in_specs, args = [row_spec], [x]
    if pending:
        in_specs += [row_spec, mod_spec]
        args += list(pending)
    in_specs += [
        mod_spec,
        vec_spec,
        vec_spec,
        pl.BlockSpec(w_in.shape, lambda i: (0, 0), **const),
        pl.BlockSpec((1, sd), lambda i: (0, 0)),
        pl.BlockSpec(ws.shape, lambda i: (0, 0, 0), **const),
        pl.BlockSpec(bs.shape, lambda i: (0, 0, 0), **const),
        pl.BlockSpec(w_out.shape, lambda i: (0, 0), **const),
    ]
    args += [mods, g.reshape(1, d), gf.reshape(1, d), w_in, norm_g.reshape(1, sd), ws, bs, w_out]
    return pl.pallas_call(
        functools.partial(_sgu_kernel, pending=bool(pending)),
        grid=(t // tm,),
        in_specs=in_specs,
        out_specs=[row_spec, row_spec],
        out_shape=[jax.ShapeDtypeStruct((t, d), jnp.float32), jax.ShapeDtypeStruct((t, d), MXU_DTYPE)],
        scratch_shapes=[pltpu.VMEM((tm, d), MXU_DTYPE), pltpu.VMEM((tm, sd), jnp.float32),
                        pltpu.VMEM((tm, sd), MXU_DTYPE)],
        compiler_params=_params(("parallel",), est),
        name="sgu_mixer",
    )(*args)


def _rope_tables(n_tokens):
    rows = n_tokens // GRID_W
    row = jnp.broadcast_to(jnp.arange(rows, dtype=jnp.float32)[:, None], (rows, GRID_W)).reshape(-1)
    col = jnp.broadcast_to(jnp.arange(GRID_W, dtype=jnp.float32)[None, :], (rows, GRID_W)).reshape(-1)
    inv = ROPE_BASE ** (-jnp.arange(ROPE_HALF, dtype=jnp.float32) / ROPE_HALF)
    ang = jnp.stack([row[:, None] * inv, col[:, None] * inv], axis=1)
    cos, sin = jnp.cos(ang), jnp.sin(ang)
    c = jnp.stack([cos, cos], axis=2).reshape(n_tokens, HEAD_DIM)
    s = jnp.stack([-sin, sin], axis=2).reshape(n_tokens, HEAD_DIM)
    return c, s


def _qkv_kernel(*refs, pending, rope, keep_f32, q_scale):
    (x_ref,), fs, modp_ref, rest = _split_pending(list(refs), pending)
    mod_ref, g_ref, w_ref = rest[:3]
    rest = rest[3:]
    if rope:
        c_ref, s_ref = rest[:2]
        rest = rest[2:]
    q_ref, k_ref, v_ref = rest[:3]
    rest = rest[3:]
    if keep_f32:
        kf_ref, vf_ref = rest
    tm, d = x_ref.shape

    def rotate(dst_ref, y, mult):
        c = c_ref[...]
        s = s_ref[...]
        lane = lax.broadcasted_iota(jnp.int32, (tm, HEAD_DIM), 1)
        low_half = (lane % (2 * ROPE_HALF)) < ROPE_HALF
        for gi in range(d // HEAD_DIM):
            cols = slice(gi * HEAD_DIM, (gi + 1) * HEAD_DIM)
            yg = y[:, cols]
            partner = jnp.where(low_half,
                                pltpu.roll(yg, HEAD_DIM - ROPE_HALF, 1),
                                pltpu.roll(yg, ROPE_HALF, 1))
            r = yg * c + partner * s
            if mult != 1.0:
                r = r * mult
            dst_ref[:, cols] = r.astype(dst_ref.dtype)

    x = _stream_rows(x_ref, fs[0] if pending else None, modp_ref)
    h = _norm_mod(x, g_ref[...], mod_ref[0:1, :], mod_ref[1:2, :]).astype(MXU_DTYPE)

    y = _dot(h, w_ref[:, 0:d])
    if rope:
        rotate(q_ref, y, q_scale)
    else:
        q_ref[...] = (y * q_scale).astype(q_ref.dtype)

    y = _dot(h, w_ref[:, d:2 * d])
    if keep_f32:
        kf_ref[...] = y
    if rope:
        rotate(k_ref, y, 1.0)
    else:
        k_ref[...] = y.astype(k_ref.dtype)

    y = _dot(h, w_ref[:, 2 * d:3 * d])
    if keep_f32:
        vf_ref[...] = y
    v_ref[...] = y.astype(v_ref.dtype)


def _qkv(x, pending, mods, mod_row, g, w_qkv, seq_len, rope, keep_f32):
    t, d = x.shape
    tm = _largest_tile(seq_len, 256, 16)
    bps = seq_len // tm
    q_scale = HEAD_DIM ** -0.5 * LOG2E
    row_spec = pl.BlockSpec((tm, d), lambda i: (i, 0))
    mod_spec = pl.BlockSpec((None, N_MOD, d), lambda i: (mod_row(i, tm), 0, 0))
    in_specs, args = [row_spec], [x]
    if pending:
        in_specs += [row_spec, mod_spec]
        args += list(pending)
    in_specs += [mod_spec, pl.BlockSpec((1, d), lambda i: (0, 0)),
                 pl.BlockSpec(w_qkv.shape, lambda i: (0, 0), pipeline_mode=pl.Buffered(1))]
    args += [mods, g.reshape(1, d), w_qkv]
    if rope:
        c, s = _rope_tables(seq_len)
        in_specs += [pl.BlockSpec((tm, HEAD_DIM), lambda i: (i % bps, 0))] * 2
        args += [c, s]
    out_specs = [row_spec] * 3
    out_shape = [jax.ShapeDtypeStruct((t, d), MXU_DTYPE)] * 3
    if keep_f32:
        out_specs += [row_spec] * 2
        out_shape += [jax.ShapeDtypeStruct((t, d), jnp.float32)] * 2
    est = w_qkv.size * 2 + (4 if pending else 2) * tm * d * 4 + 7 * tm * d * 2 + 4 * tm * d * 4 \
        + (4 * tm * d * 4 if keep_f32 else 0)
    return pl.pallas_call(
        functools.partial(_qkv_kernel, pending=bool(pending), rope=rope, keep_f32=keep_f32, q_scale=q_scale),
        grid=(t // tm,),
        in_specs=in_specs,
        out_specs=out_specs,
        out_shape=out_shape,
        compiler_params=_params(("parallel",), est),
        name="attn_qkv",
    )(*args)


def _attn_kernel(q_ref, k_ref, v_ref, *rest, lam_init, has_cache):
    if has_cache:
        ck_ref, cv_ref, lp_ref, sg_ref, o_ref = rest
        cached_v = cv_ref[...].astype(MXU_DTYPE)
    else:
        lp_ref, sg_ref, o_ref = rest
    n_new = k_ref.shape[0]
    chunk = min(KV_CHUNK, n_new)
    lp = lp_ref[...]
    lam = (jnp.exp(jnp.sum(lp[0:1, :] * lp[1:2, :], axis=-1, keepdims=True))
           - jnp.exp(jnp.sum(lp[2:3, :] * lp[3:4, :], axis=-1, keepdims=True)) + lam_init)
    for hd in range(q_ref.shape[1] // V_DIM):
        vcols = slice(hd * V_DIM, (hd + 1) * V_DIM)
        outs = []
        for mp in range(2):
            cols = slice(hd * V_DIM + mp * HEAD_DIM, hd * V_DIM + (mp + 1) * HEAD_DIM)
            q = q_ref[:, cols]
            kv = []
            if has_cache:
                kv.append((ck_ref[:, cols].astype(MXU_DTYPE), cached_v[:, vcols]))
            for c0 in range(0, n_new, chunk):
                kv.append((k_ref[c0:c0 + chunk, cols], v_ref[c0:c0 + chunk, vcols]))
            m = denom = acc = None
            for kc, vc in kv:
                s = lax.dot_general(q, kc, (((1,), (1,)), ((), ())), preferred_element_type=jnp.float32)
                mc = jnp.max(s, axis=-1, keepdims=True)
                if m is None:
                    m = mc
                    p = jnp.exp2(s - m)
                    denom = jnp.sum(p, axis=-1, keepdims=True)
                    acc = _dot(p.astype(MXU_DTYPE), vc)
                else:
                    m_new = jnp.maximum(m, mc)
                    alpha = jnp.exp2(m - m_new)
                    p = jnp.exp2(s - m_new)
                    denom = alpha * denom + jnp.sum(p, axis=-1, keepdims=True)
                    acc = alpha * acc + _dot(p.astype(MXU_DTYPE), vc)
                    m = m_new
            outs.append(acc / denom)
        o = outs[0] - lam * outs[1]
        ms = jnp.mean(o * o, axis=-1, keepdims=True)
        o = (o * lax.rsqrt(ms + EPS) * sg_ref[...]) * (1.0 - lam_init)
        o_ref[:, vcols] = o.astype(o_ref.dtype)


def _attention(q, k, v, cache, lam_params, subln_g, lam_init):
    b, n, d = q.shape
    heads = d // V_DIM
    tq = _largest_tile(n, Q_TILE, 16)
    past = 0 if cache is None else cache[0].shape[2]
    hw = d if n + past <= KV_CHUNK else V_DIM
    in_specs = [
        pl.BlockSpec((None, tq, hw), lambda bi, hi, qi: (bi, qi, hi)),
        pl.BlockSpec((None, n, hw), lambda bi, hi, qi: (bi, 0, hi)),
        pl.BlockSpec((None, n, hw), lambda bi, hi, qi: (bi, 0, hi)),
    ]
    args = [q, k, v]
    if cache is not None:
        cache_k, cache_v, slot = cache
        in_specs += [pl.BlockSpec((None, None, past, hw), lambda bi, hi, qi: (bi, slot, 0, hi))] * 2
        args += [cache_k, cache_v]
    in_specs += [
        pl.BlockSpec(lam_params.shape, lambda bi, hi, qi: (0, 0)),
        pl.BlockSpec((1, V_DIM), lambda bi, hi, qi: (0, 0)),
    ]
    args += [lam_params, subln_g.reshape(1, V_DIM)]
    chunk = min(KV_CHUNK, n)
    est = 4 * tq * hw * 2 + 4 * n * hw * 2 + 4 * past * hw * 4 \
        + (hw // V_DIM) * (6 * tq * chunk * 4 + 8 * tq * V_DIM * 4)
    return pl.pallas_call(
        functools.partial(_attn_kernel, lam_init=lam_init, has_cache=cache is not None),
        grid=(b, d // hw, n // tq),
        in_specs=in_specs,
        out_specs=pl.BlockSpec((None, tq, hw), lambda bi, hi, qi: (bi, qi, hi)),
        out_shape=jax.ShapeDtypeStruct((b, n, d), MXU_DTYPE),
        compiler_params=_params(("parallel", "parallel", "parallel"), est),
        name="diff_attention",
    )(*args)


def _proj_kernel(*refs, pending):
    (x_ref,), fs, modp_ref, rest = _split_pending(list(refs), pending)
    a_ref, w_ref, mod_ref, gf_ref, o_ref, h2_ref = rest
    half = x_ref.shape[0] // 2
    for r0 in (0, half):
        rows = slice(r0, r0 + half)
        x = _stream_rows(x_ref.at[rows], fs[0].at[rows] if pending else None, modp_ref)
        x_new = x + mod_ref[2:3, :] * _dot(a_ref[rows, :], w_ref[...])
        o_ref[rows, :] = x_new
        _emit_ffn_input(h2_ref.at[rows], x_new, gf_ref, mod_ref)


def _attn_out_proj(x, pending, a, w_o, mods, mod_row, gf):
    t, d = x.shape
    tm = _largest_tile(t, 512, 16)
    est = 4 * tm * d * 2 + 2 * d * d * 2 + (8 if pending else 6) * tm * d * 4
    row_spec = pl.BlockSpec((tm, d), lambda i: (i, 0))
    mod_spec = pl.BlockSpec((None, N_MOD, d), lambda i: (mod_row(i, tm), 0, 0))
    in_specs, args = [row_spec], [x]
    if pending:
        in_specs += [row_spec, mod_spec]
        args += list(pending)
    in_specs += [row_spec, pl.BlockSpec((d, d), lambda i: (0, 0)), mod_spec, pl.BlockSpec((1, d), lambda i: (0, 0))]
    args += [a, w_o, mods, gf.reshape(1, d)]
    return pl.pallas_call(
        functools.partial(_proj_kernel, pending=bool(pending)),
        grid=(t // tm,),
        in_specs=in_specs,
        out_specs=[row_spec, row_spec],
        out_shape=[jax.ShapeDtypeStruct((t, d), jnp.float32), jax.ShapeDtypeStruct((t, d), MXU_DTYPE)],
        compiler_params=_params(("parallel",), est),
        name="attn_out_proj",
    )(*args)


def kernel(x_prompt, x_sample, cache_k, cache_v, c, c_ctx, ada_w, ada_b, norm_mix_g, norm_ffn_g,
           pool_w, pool_scale, sgu_w_in, sgu_norm_g, sgu_ws, sgu_b, sgu_w_out,
           attn_w_qkv, attn_lambda, attn_subln_g, attn_w_o, ffn_w_in, ffn_w_out, final_g):
    batch, seq, d = x_prompt.shape
    dec_batch, dec_seq, _ = x_sample.shape
    depth = ada_w.shape[0]
    assert dec_batch < COND_ROWS and d % V_DIM == 0 and dec_seq % GRID_W == 0

    cond = jnp.zeros((COND_ROWS, d), jnp.float32).at[:dec_batch].set(c).at[dec_batch].set(c_ctx)
    mods = _ada_mods(cond, ada_w, ada_b).reshape(depth, COND_ROWS, N_MOD, d)

    cast = lambda w: w.astype(MXU_DTYPE)
    hidden = ffn_w_out.shape[1]
    _, _, ni, nj = _ffn_grid(dec_batch * dec_seq, hidden)
    side_cast = depth > 1 and all(_cast_plan(w.shape, ni, nj) for w in (ffn_w_in, ffn_w_out))
    if side_cast:
        ffn_w = [(cast(ffn_w_in[:1]), cast(ffn_w_out[:1]), 0)]
    else:
        w_in_all, w_out_all = cast(ffn_w_in), cast(ffn_w_out)
        ffn_w = [(w_in_all, w_out_all, l) for l in range(depth)]
    cache_k = cache_k.reshape(cache_k.shape[:3] + (d,))
    cache_v = cache_v.reshape(cache_v.shape[:3] + (d,))
    streams = {
        "ctx": (x_prompt.reshape(batch * seq, d), seq, lambda i, tm: dec_batch),
        "lat": (x_sample.reshape(dec_batch * dec_seq, d), dec_seq,
                lambda i, tm: (i * tm) // dec_seq),
    }
    xs = {name: s[0] for name, s in streams.items()}
    pend = {name: None for name in streams}
    new_k, new_v = [], []
    for i in range(depth):
        kind, slot = i % N_MIXERS, i // N_MIXERS
        m_i = mods[i]
        gf = norm_ffn_g[i]
        last = i == depth - 1
        for name, (_, n, mod_row) in streams.items():
            x, pending = xs[name], pend[name]
            if kind == 0:
                x, h2 = _pool_mixer(x, pending, m_i, mod_row, norm_mix_g[i], gf, cast(pool_w[slot]),
                                    pool_scale[slot], n)
            elif kind == 1:
                x, h2 = _sgu_mixer(x, pending, m_i, mod_row, norm_mix_g[i], gf, cast(sgu_w_in[slot]),
                                   sgu_norm_g[slot], cast(sgu_ws[slot]), sgu_b[slot][:, :, None],
                                   cast(sgu_w_out[slot]))
            else:
                lam_init = 0.8 - 0.6 * math.exp(-0.3 * i)
                is_ctx = name == "ctx"
                outs = _qkv(x, pending, m_i, mod_row, norm_mix_g[i], cast(attn_w_qkv[slot]), n,
                            rope=not is_ctx, keep_f32=is_ctx)
                nb = x.shape[0] // n
                q, k, v = (a.reshape(nb, n, d) for a in outs[:3])
                if is_ctx:
                    new_k.append(outs[3])
                    new_v.append(outs[4])
                a = _attention(q, k, v, None if is_ctx else (cache_k, cache_v, slot),
                               attn_lambda[slot], attn_subln_g[slot], lam_init)
                x, h2 = _attn_out_proj(x, pending, a.reshape(-1, d), cast(attn_w_o[slot]), m_i, mod_row, gf)
            if last:
                xs[name] = _ffn(h2, *ffn_w[i], residual=(x, m_i, mod_row), final_g=final_g)
                pend[name] = None
            elif side_cast and i == 0 and name == "lat":
                f, w_in_rest, w_out_rest = _ffn(h2, *ffn_w[0], casts=(ffn_w_in, ffn_w_out))
                ffn_w += [(w_in_rest.reshape((depth - 1,) + ffn_w_in.shape[1:]),
                           w_out_rest.reshape((depth - 1,) + ffn_w_out.shape[1:]), l - 1)
                          for l in range(1, depth)]
                xs[name], pend[name] = x, (f, m_i)
            else:
                xs[name], pend[name] = x, (_ffn(h2, *ffn_w[i]), m_i)
    heads = d // V_DIM
    y_prompt = xs["ctx"].reshape(batch, seq, d)
    y_sample = xs["lat"].reshape(dec_batch, dec_seq, d)
    new_cache_k = jnp.stack([a.reshape(batch, seq, heads, 2, HEAD_DIM) for a in new_k], axis=1)
    new_cache_v = jnp.stack([a.reshape(batch, seq, heads, V_DIM) for a in new_v], axis=1)
    return (y_prompt, y_sample, new_cache_k, new_cache_v)
```

```python
import functools
import math

import jax
import jax.numpy as jnp
from jax import lax
from jax.experimental import pallas as pl
from jax.experimental.pallas import tpu as pltpu

EPS = 1e-6
N_MOD = 6
N_MIXERS = 3
POOL_WINDOWS = (2, 4, 8, 16)
POOL_HALO = 8
SGU_GROUPS = 8
SGU_CHUNK = 128
HEAD_DIM = 128
V_DIM = 2 * HEAD_DIM
GRID_W = 64
ROPE_BASE = 10000.0
ROPE_HALF = HEAD_DIM // 4
LOG2E = 1.4426950408889634
MXU_DTYPE = jnp.bfloat16
LANES = 128
PACKED_ROWS = 16
COND_ROWS = PACKED_ROWS
V7X_VMEM_BYTES = 64 * 1024 * 1024
VMEM_CAP_BYTES = V7X_VMEM_BYTES - 8 * 1024 * 1024
VMEM_FLOOR_BYTES = 16 * 1024 * 1024

ROW_TILE = 512
FFN_ROW_TILE = 1024
FFN_HIDDEN_TILE = 512
RESIDENT_WEIGHT_ROW_TILE = 256
ADA_COL_TILE = 1024
KV_CHUNK = 512
Q_TILE = 1024


def _vmem_limit(estimate_bytes):
    return int(min(max(estimate_bytes * 5 // 4, VMEM_FLOOR_BYTES), VMEM_CAP_BYTES))


def _params(semantics, vmem_estimate):
    return pltpu.CompilerParams(dimension_semantics=semantics,
                                vmem_limit_bytes=_vmem_limit(vmem_estimate))


def _dot(a, b):
    return jnp.dot(a, b, preferred_element_type=jnp.float32)


def _norm_mod(x, g, shift, scale):
    ms = jnp.mean(x * x, axis=-1, keepdims=True)
    return (x * lax.rsqrt(ms + EPS)) * (g * (1.0 + scale)) + shift


def _emit_ffn_input(h2_ref, x_new, gf_ref, mod_ref):
    h2_ref[...] = _norm_mod(x_new, gf_ref[...], mod_ref[3:4, :], mod_ref[4:5, :]).astype(h2_ref.dtype)


def _split_pending(refs, pending, n=1):
    xs, rest = refs[:n], refs[n:]
    if not pending:
        return xs, None, None, rest
    return xs, rest[:n], rest[n], rest[n + 1:]


def _stream_rows(x_ref, f_ref, modp_ref):
    x = x_ref[...]
    if f_ref is not None:
        x = x + modp_ref[5:6, :] * f_ref[...]
    return x


def _largest_tile(n, cap, quantum):
    t = min(cap, n)
    t -= t % quantum
    while n % t:
        t -= quantum
    return t


def _ada_kernel(cond_ref, w_ref, b_ref, o_ref):
    c = cond_ref[...]
    s = (c * jax.nn.sigmoid(c)).astype(MXU_DTYPE)
    o_ref[...] = _dot(s, w_ref[...].astype(MXU_DTYPE)) + b_ref[...]


def _ada_mods(cond, ada_w, ada_b):
    depth, d, n = ada_w.shape
    bn = _largest_tile(n, ADA_COL_TILE, LANES)
    est = 2 * d * bn * 4 + d * bn * 2 + 4 * COND_ROWS * (d + bn) * 4
    return pl.pallas_call(
        _ada_kernel,
        grid=(depth, n // bn),
        in_specs=[
            pl.BlockSpec((COND_ROWS, d), lambda l, j: (0, 0)),
            pl.BlockSpec((None, d, bn), lambda l, j: (l, 0, j)),
            pl.BlockSpec((None, 1, bn), lambda l, j: (l, 0, j)),
        ],
        out_specs=pl.BlockSpec((None, COND_ROWS, bn), lambda l, j: (l, 0, j)),
        out_shape=jax.ShapeDtypeStruct((depth, COND_ROWS, n), jnp.float32),
        compiler_params=_params(("parallel", "parallel"), est),
        name="ada_mods",
    )(cond, ada_w, ada_b.reshape(depth, 1, n))


def _ffn_kernel(*refs, n_hidden_blocks, residual, final_norm, n_casts):
    refs = list(refs)
    if residual:
        x_ref, mod_ref = refs[:2]
        refs = refs[2:]
    h_ref, wa_ref, wb_ref, wo_ref = refs[:4]
    refs = refs[4:]
    if final_norm:
        fg_ref = refs.pop(0)
    cast_src, refs = refs[:n_casts], refs[n_casts:]
    o_ref, cast_dst = refs[0], refs[1:]
    j = pl.program_id(1)

    @pl.when(j == 0)
    def _():
        o_ref[...] = jnp.zeros_like(o_ref)

    for src, dst in zip(cast_src, cast_dst):
        dst[...] = src[...].astype(dst.dtype)

    h = h_ref[...]
    a = _dot(h, wa_ref[...])
    b = _dot(h, wb_ref[...])
    act = (a * jax.nn.sigmoid(a)) * b
    o_ref[...] += _dot(act.astype(MXU_DTYPE), wo_ref[...])

    if residual:
        @pl.when(j == n_hidden_blocks - 1)
        def _():
            o_ref[...] = x_ref[...] + mod_ref[5:6, :] * o_ref[...]
            if final_norm:
                y = o_ref[...]
                ms = jnp.mean(y * y, axis=-1, keepdims=True)
                o_ref[...] = (o_ref[...] * lax.rsqrt(ms + EPS)) * fg_ref[...]


def _cast_plan(shape, ni, nj):
    layers, r, c = shape
    rest = (layers - 1) * r

    def rows_per_block(max_blocks):
        return next((br for br in range(PACKED_ROWS, r + 1, PACKED_ROWS)
                     if r % br == 0 and rest // br <= max_blocks), None)

    if c % nj == 0 and (c // nj) % LANES == 0 and rows_per_block(ni):
        br = rows_per_block(ni)
        last, first = rest // br - 1, r // br
        return (pl.BlockSpec((br, c // nj), lambda i, j: (first + jnp.minimum(i, last), j)),
                pl.BlockSpec((br, c // nj), lambda i, j: (jnp.minimum(i, last), j)), (rest, c))
    br = rows_per_block(ni * nj)
    if br is None:
        return None
    last, first = rest // br - 1, r // br
    return (pl.BlockSpec((br, c), lambda i, j: (first + jnp.minimum(i * nj + j, last), 0)),
            pl.BlockSpec((br, c), lambda i, j: (jnp.minimum(i * nj + j, last), 0)), (rest, c))


def _ffn_grid(t, f, fused_residual=False):
    tm = _largest_tile(t, ROW_TILE if fused_residual else FFN_ROW_TILE, PACKED_ROWS)
    th = _largest_tile(f, FFN_HIDDEN_TILE, LANES)
    return tm, th, t // tm, f // th


def _ffn(h, w_in, w_out, layer, residual=None, final_g=None, casts=()):
    t, d = h.shape
    f = w_out.shape[1]
    fused = residual is not None
    tm, th, ni, nj = _ffn_grid(t, f, fused)
    row_spec = pl.BlockSpec((tm, d), lambda i, j: (i, 0))
    in_specs, args = [], []
    if fused:
        x, mods, mod_row = residual
        in_specs += [row_spec, pl.BlockSpec((None, N_MOD, d), lambda i, j: (mod_row(i, tm), 0, 0))]
        args += [x, mods]
    in_specs += [
        row_spec,
        pl.BlockSpec((None, d, th), lambda i, j: (layer, 0, j)),
        pl.BlockSpec((None, d, th), lambda i, j: (layer, 0, j + nj)),
        pl.BlockSpec((None, th, d), lambda i, j: (layer, j, 0)),
    ]
    args += [h, w_in, w_in, w_out]
    if final_g is not None:
        in_specs.append(pl.BlockSpec((1, d), lambda i, j: (0, 0)))
        args.append(final_g.reshape(1, d))
    plans = [_cast_plan(a.shape, ni, nj) for a in casts]
    in_specs += [p[0] for p in plans]
    args += [a.reshape(-1, a.shape[-1]) for a in casts]
    est = (4 if fused else 2) * tm * d * 4 + 2 * tm * d * 2 + 6 * d * th * 2 + 7 * tm * th * 4 \
        + sum(2 * 6 * math.prod(p[0].block_shape) for p in plans)
    out = pl.pallas_call(
        functools.partial(_ffn_kernel, n_hidden_blocks=nj, residual=fused, final_norm=final_g is not None,
                          n_casts=len(casts)),
        grid=(ni, nj),
        in_specs=in_specs,
        out_specs=[row_spec] + [p[1] for p in plans],
        out_shape=[jax.ShapeDtypeStruct((t, d), jnp.float32)]
        + [jax.ShapeDtypeStruct(p[2], MXU_DTYPE) for p in plans],
        compiler_params=_params(("parallel", "arbitrary"), est),
        name="ffn",
    )(*args)
    return out if casts else out[0]


def _pool_kernel(*refs, seq_len, blocks_per_seq, pending):
    (x_ref, xp_ref, xn_ref), fs, modp_ref, rest = _split_pending(list(refs), pending, 3)
    f_ref, fp_ref, fn_ref = fs if pending else (None, None, None)
    mod_ref, g_ref, gf_ref, w_ref, sc_ref, o_ref, h2_ref, h_sc, p_sc, d_sc = rest
    tm, d = x_ref.shape
    gd = d // len(POOL_WINDOWS)
    n = tm + 2 * POOL_HALO
    i = pl.program_id(0)
    blk = i % blocks_per_seq
    g = g_ref[...]
    shift = mod_ref[0:1, :]
    scale = mod_ref[1:2, :]
    x = _stream_rows(x_ref, f_ref, modp_ref)
    has_prev = (blk > 0).astype(jnp.float32)
    has_next = (blk < blocks_per_seq - 1).astype(jnp.float32)
    h_sc[0:POOL_HALO, :] = _norm_mod(_stream_rows(xp_ref, fp_ref, modp_ref), g, shift, scale) * has_prev
    h_sc[POOL_HALO:POOL_HALO + tm, :] = _norm_mod(x, g, shift, scale)
    h_sc[POOL_HALO + tm:n, :] = _norm_mod(_stream_rows(xn_ref, fn_ref, modp_ref), g, shift, scale) * has_next
    h_sc[n:, :] = jnp.zeros((POOL_HALO, d), jnp.float32)
    p_sc[n:, :] = jnp.zeros((POOL_HALO, gd), jnp.float32)

    pos = blk * tm + lax.broadcasted_iota(jnp.int32, (tm, 1), 0)
    for gi, win in enumerate(POOL_WINDOWS):
        cols = slice(gi * gd, (gi + 1) * gd)
        half = win // 2
        if half == 1:
            lo = h_sc[POOL_HALO - 1:POOL_HALO - 1 + tm, cols]
            hi = h_sc[POOL_HALO:POOL_HALO + tm, cols]
        else:
            p_sc[0:n, :] = h_sc[0:n, cols] + h_sc[1:n + 1, cols]
            s = 2
            while s < half:
                p_sc[0:n, :] = p_sc[0:n, :] + p_sc[s:n + s, :]
                s *= 2
            lo = p_sc[POOL_HALO - half:POOL_HALO - half + tm, :]
            hi = p_sc[POOL_HALO:POOL_HALO + tm, :]
        cnt = jnp.minimum(pos + half, seq_len) - jnp.maximum(pos - half, 0)
        inv_cnt = 1.0 / cnt.astype(jnp.float32)
        dev = (lo + hi) * inv_cnt - h_sc[POOL_HALO:POOL_HALO + tm, cols]
        d_sc[:, cols] = _dot(dev.astype(MXU_DTYPE), w_ref[gi])
    x_new = x + d_sc[...] * (mod_ref[2:3, :] * sc_ref[...])
    o_ref[...] = x_new
    _emit_ffn_input(h2_ref, x_new, gf_ref, mod_ref)


def _pool_mixer(x, pending, mods, mod_row, g, gf, w, scale, seq_len):
    t, d = x.shape
    tm = _largest_tile(seq_len, ROW_TILE, PACKED_ROWS)
    bps = seq_len // tm
    hb = tm // POOL_HALO
    n_halo_blocks = t // POOL_HALO
    gd = d // len(POOL_WINDOWS)
    row_spec = pl.BlockSpec((tm, d), lambda i: (i, 0))
    vec_spec = pl.BlockSpec((1, d), lambda i: (0, 0))
    mod_spec = pl.BlockSpec((None, N_MOD, d), lambda i: (mod_row(i, tm), 0, 0))
    halo_specs = [
        row_spec,
        pl.BlockSpec((POOL_HALO, d), lambda i: (jnp.maximum(i * hb - 1, 0), 0)),
        pl.BlockSpec((POOL_HALO, d), lambda i: (jnp.minimum((i + 1) * hb, n_halo_blocks - 1), 0)),
    ]
    in_specs, args = list(halo_specs), [x, x, x]
    if pending:
        f, mods_prev = pending
        in_specs += halo_specs + [mod_spec]
        args += [f, f, f, mods_prev]
    in_specs += [mod_spec, vec_spec, vec_spec, pl.BlockSpec(w.shape, lambda i: (0, 0, 0)), vec_spec]
    args += [mods, g.reshape(1, d), gf.reshape(1, d), w, scale.reshape(1, d)]
    est = (6 if pending else 4) * tm * d * 4 + 2 * tm * d * 2 + 2 * (tm + 3 * POOL_HALO) * d * 4 \
        + 2 * w.size * 2 + 6 * tm * gd * 4
    return pl.pallas_call(
        functools.partial(_pool_kernel, seq_len=seq_len, blocks_per_seq=bps, pending=bool(pending)),
        grid=(t // tm,),
        in_specs=in_specs,
        out_specs=[row_spec, row_spec],
        out_shape=[jax.ShapeDtypeStruct((t, d), jnp.float32), jax.ShapeDtypeStruct((t, d), MXU_DTYPE)],
        scratch_shapes=[pltpu.VMEM((tm + 3 * POOL_HALO, d), jnp.float32),
                        pltpu.VMEM((tm + 3 * POOL_HALO, gd), jnp.float32),
                        pltpu.VMEM((tm, d), jnp.float32)],
        compiler_params=_params(("parallel",), est),
        name="pool_mixer",
    )(*args)


def _sgu_kernel(*refs, pending):
    (x_ref,), fs, modp_ref, rest = _split_pending(list(refs), pending)
    mod_ref, g_ref, gf_ref, win_ref, ng_ref, ws_ref, bs_ref, wout_ref, o_ref, h2_ref, h_sc, v_sc, p_sc = rest
    tm, d = x_ref.shape
    sd = wout_ref.shape[0]
    gd = sd // SGU_GROUPS
    x = _stream_rows(x_ref, fs[0] if pending else None, modp_ref)
    h_sc[...] = _norm_mod(x, g_ref[...], mod_ref[0:1, :], mod_ref[1:2, :]).astype(MXU_DTYPE)
    h = h_sc[...]

    def gelu(z):
        return 0.5 * z * (1.0 + lax.erf(z * (1.0 / math.sqrt(2.0))))

    ssq = jnp.zeros((tm, 1), jnp.float32)
    for gi in range(SGU_GROUPS):
        v = gelu(_dot(h, win_ref[:, sd + gi * gd:sd + (gi + 1) * gd]))
        ssq = ssq + jnp.sum(v * v, axis=-1, keepdims=True)
        v_sc[:, gi * gd:(gi + 1) * gd] = v
    rstd = lax.rsqrt(ssq * (1.0 / sd) + EPS)

    for gi in range(SGU_GROUPS):
        cols = slice(gi * gd, (gi + 1) * gd)
        vn = (v_sc[:, cols] * rstd * ng_ref[:, cols]).astype(MXU_DTYPE)
        u = gelu(_dot(h, win_ref[:, cols]))
        wsg = ws_ref[gi]
        bias = bs_ref[gi]
        for c in range(tm // SGU_CHUNK):
            rows = slice(c * SGU_CHUNK, (c + 1) * SGU_CHUNK)
            vm = _dot(wsg, vn[rows, :]) + bias
            p_sc[rows, cols] = (u[rows, :] * vm).astype(MXU_DTYPE)
    x_new = x + mod_ref[2:3, :] * _dot(p_sc[...], wout_ref[...])
    o_ref[...] = x_new
    _emit_ffn_input(h2_ref, x_new, gf_ref, mod_ref)


def _sgu_mixer(x, pending, mods, mod_row, g, gf, w_in, norm_g, ws, bs, w_out):
    t, d = x.shape
    sd = w_out.shape[0]
    tm = _largest_tile(t, RESIDENT_WEIGHT_ROW_TILE, SGU_CHUNK)
    est = (w_in.size + w_out.size + ws.size) * 2 + (6 if pending else 4) * tm * d * 4 + tm * sd * 4 \
        + 4 * tm * d * 2 + 6 * tm * (sd // SGU_GROUPS) * 4
    const = dict(pipeline_mode=pl.Buffered(1))
    row_spec = pl.BlockSpec((tm, d), lambda i: (i, 0))
    vec_spec = pl.BlockSpec((1, d), lambda i: (0, 0))
    mod_spec = pl.BlockSpec((None, N_MOD, d), lambda i: (mod_row(i, tm), 0, 0))
    in_specs, args = [row_spec], [x]
    if pending:
        in_specs += [row_spec, mod_spec]
        args += list(pending)
    in_specs += [
        mod_spec,
        vec_spec,
        vec_spec,
        pl.BlockSpec(w_in.shape, lambda i: (0, 0), **const),
        pl.BlockSpec((1, sd), lambda i: (0, 0)),
        pl.BlockSpec(ws.shape, lambda i: (0, 0, 0), **const),
        pl.BlockSpec(bs.shape, lambda i: (0, 0, 0), **const),
        pl.BlockSpec(w_out.shape, lambda i: (0, 0), **const),
    ]
    args += [mods, g.reshape(1, d), gf.reshape(1, d), w_in, norm_g.reshape(1, sd), ws, bs, w_out]
    return pl.pallas_call(
        functools.partial(_sgu_kernel, pending=bool(pending)),
        grid=(t // tm,),
        in_specs=in_specs,
        out_specs=[row_spec, row_spec],
        out_shape=[jax.ShapeDtypeStruct((t, d), jnp.float32), jax.ShapeDtypeStruct((t, d), MXU_DTYPE)],
        scratch_shapes=[pltpu.VMEM((tm, d), MXU_DTYPE), pltpu.VMEM((tm, sd), jnp.float32),
                        pltpu.VMEM((tm, sd), MXU_DTYPE)],
        compiler_params=_params(("parallel",), est),
        name="sgu_mixer",
    )(*args)


def _rope_tables(n_tokens):
    rows = n_tokens // GRID_W
    row = jnp.broadcast_to(jnp.arange(rows, dtype=jnp.float32)[:, None], (rows, GRID_W)).reshape(-1)
    col = jnp.broadcast_to(jnp.arange(GRID_W, dtype=jnp.float32)[None, :], (rows, GRID_W)).reshape(-1)
    inv = ROPE_BASE ** (-jnp.arange(ROPE_HALF, dtype=jnp.float32) / ROPE_HALF)
    ang = jnp.stack([row[:, None] * inv, col[:, None] * inv], axis=1)
    cos, sin = jnp.cos(ang), jnp.sin(ang)
    c = jnp.stack([cos, cos], axis=2).reshape(n_tokens, HEAD_DIM)
    s = jnp.stack([-sin, sin], axis=2).reshape(n_tokens, HEAD_DIM)
    return c, s


def _qkv_kernel(*refs, pending, rope, keep_f32, q_scale):
    (x_ref,), fs, modp_ref, rest = _split_pending(list(refs), pending)
    mod_ref, g_ref, w_ref = rest[:3]
    rest = rest[3:]
    if rope:
        c_ref, s_ref = rest[:2]
        rest = rest[2:]
    q_ref, k_ref, v_ref = rest[:3]
    rest = rest[3:]
    if keep_f32:
        kf_ref, vf_ref = rest
    tm, d = x_ref.shape

    def rotate(dst_ref, y, mult):
        c = c_ref[...]
        s = s_ref[...]
        lane = lax.broadcasted_iota(jnp.int32, (tm, HEAD_DIM), 1)
        low_half = (lane % (2 * ROPE_HALF)) < ROPE_HALF
        for gi in range(d // HEAD_DIM):
            cols = slice(gi * HEAD_DIM, (gi + 1) * HEAD_DIM)
            yg = y[:, cols]
            partner = jnp.where(low_half,
                                pltpu.roll(yg, HEAD_DIM - ROPE_HALF, 1),
                                pltpu.roll(yg, ROPE_HALF, 1))
            r = yg * c + partner * s
            if mult != 1.0:
                r = r * mult
            dst_ref[:, cols] = r.astype(dst_ref.dtype)

    x = _stream_rows(x_ref, fs[0] if pending else None, modp_ref)
    h = _norm_mod(x, g_ref[...], mod_ref[0:1, :], mod_ref[1:2, :]).astype(MXU_DTYPE)

    y = _dot(h, w_ref[:, 0:d])
    if rope:
        rotate(q_ref, y, q_scale)
    else:
        q_ref[...] = (y * q_scale).astype(q_ref.dtype)

    y = _dot(h, w_ref[:, d:2 * d])
    if keep_f32:
        kf_ref[...] = y
    if rope:
        rotate(k_ref, y, 1.0)
    else:
        k_ref[...] = y.astype(k_ref.dtype)

    y = _dot(h, w_ref[:, 2 * d:3 * d])
    if keep_f32:
        vf_ref[...] = y
    v_ref[...] = y.astype(v_ref.dtype)


def _qkv(x, pending, mods, mod_row, g, w_qkv, seq_len, rope, keep_f32):
    t, d = x.shape
    tm = _largest_tile(seq_len, RESIDENT_WEIGHT_ROW_TILE, PACKED_ROWS)
    bps = seq_len // tm
    q_scale = HEAD_DIM ** -0.5 * LOG2E
    row_spec = pl.BlockSpec((tm, d), lambda i: (i, 0))
    mod_spec = pl.BlockSpec((None, N_MOD, d), lambda i: (mod_row(i, tm), 0, 0))
    in_specs, args = [row_spec], [x]
    if pending:
        in_specs += [row_spec, mod_spec]
        args += list(pending)
    in_specs += [mod_spec, pl.BlockSpec((1, d), lambda i: (0, 0)),
                 pl.BlockSpec(w_qkv.shape, lambda i: (0, 0), pipeline_mode=pl.Buffered(1))]
    args += [mods, g.reshape(1, d), w_qkv]
    if rope:
        c, s = _rope_tables(seq_len)
        in_specs += [pl.BlockSpec((tm, HEAD_DIM), lambda i: (i % bps, 0))] * 2
        args += [c, s]
    out_specs = [row_spec] * 3
    out_shape = [jax.ShapeDtypeStruct((t, d), MXU_DTYPE)] * 3
    if keep_f32:
        out_specs += [row_spec] * 2
        out_shape += [jax.ShapeDtypeStruct((t, d), jnp.float32)] * 2
    est = w_qkv.size * 2 + (4 if pending else 2) * tm * d * 4 + 7 * tm * d * 2 + 4 * tm * d * 4 \
        + (4 * tm * d * 4 if keep_f32 else 0)
    return pl.pallas_call(
        functools.partial(_qkv_kernel, pending=bool(pending), rope=rope, keep_f32=keep_f32, q_scale=q_scale),
        grid=(t // tm,),
        in_specs=in_specs,
        out_specs=out_specs,
        out_shape=out_shape,
        compiler_params=_params(("parallel",), est),
        name="attn_qkv",
    )(*args)


def _attn_kernel(q_ref, k_ref, v_ref, *rest, lam_init, has_cache):
    if has_cache:
        ck_ref, cv_ref, lp_ref, sg_ref, o_ref = rest
        cached_v = cv_ref[...].astype(MXU_DTYPE)
    else:
        lp_ref, sg_ref, o_ref = rest
    n_new = k_ref.shape[0]
    chunk = min(KV_CHUNK, n_new)
    lp = lp_ref[...]
    lam = (jnp.exp(jnp.sum(lp[0:1, :] * lp[1:2, :], axis=-1, keepdims=True))
           - jnp.exp(jnp.sum(lp[2:3, :] * lp[3:4, :], axis=-1, keepdims=True)) + lam_init)
    for hd in range(q_ref.shape[1] // V_DIM):
        vcols = slice(hd * V_DIM, (hd + 1) * V_DIM)
        outs = []
        for mp in range(2):
            cols = slice(hd * V_DIM + mp * HEAD_DIM, hd * V_DIM + (mp + 1) * HEAD_DIM)
            q = q_ref[:, cols]
            kv = []
            if has_cache:
                kv.append((ck_ref[:, cols].astype(MXU_DTYPE), cached_v[:, vcols]))
            for c0 in range(0, n_new, chunk):
                kv.append((k_ref[c0:c0 + chunk, cols], v_ref[c0:c0 + chunk, vcols]))
            m = denom = acc = None
            for kc, vc in kv:
                s = lax.dot_general(q, kc, (((1,), (1,)), ((), ())), preferred_element_type=jnp.float32)
                mc = jnp.max(s, axis=-1, keepdims=True)
                if m is None:
                    m = mc
                    p = jnp.exp2(s - m)
                    denom = jnp.sum(p, axis=-1, keepdims=True)
                    acc = _dot(p.astype(MXU_DTYPE), vc)
                else:
                    m_new = jnp.maximum(m, mc)
                    alpha = jnp.exp2(m - m_new)
                    p = jnp.exp2(s - m_new)
                    denom = alpha * denom + jnp.sum(p, axis=-1, keepdims=True)
                    acc = alpha * acc + _dot(p.astype(MXU_DTYPE), vc)
                    m = m_new
            outs.append(acc / denom)
        o = outs[0] - lam * outs[1]
        ms = jnp.mean(o * o, axis=-1, keepdims=True)
        o = (o * lax.rsqrt(ms + EPS) * sg_ref[...]) * (1.0 - lam_init)
        o_ref[:, vcols] = o.astype(o_ref.dtype)


def _attention(q, k, v, cache, lam_params, subln_g, lam_init):
    b, n, d = q.shape
    tq = _largest_tile(n, Q_TILE, PACKED_ROWS)
    past = 0 if cache is None else cache[0].shape[2]
    hw = d if n + past <= KV_CHUNK else V_DIM
    in_specs = [
        pl.BlockSpec((None, tq, hw), lambda bi, hi, qi: (bi, qi, hi)),
        pl.BlockSpec((None, n, hw), lambda bi, hi, qi: (bi, 0, hi)),
        pl.BlockSpec((None, n, hw), lambda bi, hi, qi: (bi, 0, hi)),
    ]
    args = [q, k, v]
    if cache is not None:
        cache_k, cache_v, slot = cache
        in_specs += [pl.BlockSpec((None, None, past, hw), lambda bi, hi, qi: (bi, slot, 0, hi))] * 2
        args += [cache_k, cache_v]
    in_specs += [
        pl.BlockSpec(lam_params.shape, lambda bi, hi, qi: (0, 0)),
        pl.BlockSpec((1, V_DIM), lambda bi, hi, qi: (0, 0)),
    ]
    args += [lam_params, subln_g.reshape(1, V_DIM)]
    chunk = min(KV_CHUNK, n)
    est = 4 * tq * hw * 2 + 4 * n * hw * 2 + 4 * past * hw * 4 \
        + (hw // V_DIM) * (6 * tq * chunk * 4 + 8 * tq * V_DIM * 4)
    return pl.pallas_call(
        functools.partial(_attn_kernel, lam_init=lam_init, has_cache=cache is not None),
        grid=(b, d // hw, n // tq),
        in_specs=in_specs,
        out_specs=pl.BlockSpec((None, tq, hw), lambda bi, hi, qi: (bi, qi, hi)),
        out_shape=jax.ShapeDtypeStruct((b, n, d), MXU_DTYPE),
        compiler_params=_params(("parallel", "parallel", "parallel"), est),
        name="diff_attention",
    )(*args)


def _proj_kernel(*refs, pending):
    (x_ref,), fs, modp_ref, rest = _split_pending(list(refs), pending)
    a_ref, w_ref, mod_ref, gf_ref, o_ref, h2_ref = rest
    half = x_ref.shape[0] // 2
    for r0 in (0, half):
        rows = slice(r0, r0 + half)
        x = _stream_rows(x_ref.at[rows], fs[0].at[rows] if pending else None, modp_ref)
        x_new = x + mod_ref[2:3, :] * _dot(a_ref[rows, :], w_ref[...])
        o_ref[rows, :] = x_new
        _emit_ffn_input(h2_ref.at[rows], x_new, gf_ref, mod_ref)


def _attn_out_proj(x, pending, a, w_o, mods, mod_row, gf):
    t, d = x.shape
    tm = _largest_tile(t, ROW_TILE, PACKED_ROWS)
    est = 4 * tm * d * 2 + 2 * d * d * 2 + (8 if pending else 6) * tm * d * 4
    row_spec = pl.BlockSpec((tm, d), lambda i: (i, 0))
    mod_spec = pl.BlockSpec((None, N_MOD, d), lambda i: (mod_row(i, tm), 0, 0))
    in_specs, args = [row_spec], [x]
    if pending:
        in_specs += [row_spec, mod_spec]
        args += list(pending)
    in_specs += [row_spec, pl.BlockSpec((d, d), lambda i: (0, 0)), mod_spec, pl.BlockSpec((1, d), lambda i: (0, 0))]
    args += [a, w_o, mods, gf.reshape(1, d)]
    return pl.pallas_call(
        functools.partial(_proj_kernel, pending=bool(pending)),
        grid=(t // tm,),
        in_specs=in_specs,
        out_specs=[row_spec, row_spec],
        out_shape=[jax.ShapeDtypeStruct((t, d), jnp.float32), jax.ShapeDtypeStruct((t, d), MXU_DTYPE)],
        compiler_params=_params(("parallel",), est),
        name="attn_out_proj",
    )(*args)


def kernel(x_prompt, x_sample, cache_k, cache_v, c, c_ctx, ada_w, ada_b, norm_mix_g, norm_ffn_g,
           pool_w, pool_scale, sgu_w_in, sgu_norm_g, sgu_ws, sgu_b, sgu_w_out,
           attn_w_qkv, attn_lambda, attn_subln_g, attn_w_o, ffn_w_in, ffn_w_out, final_g):
    batch, seq, d = x_prompt.shape
    dec_batch, dec_seq, _ = x_sample.shape
    depth = ada_w.shape[0]
    assert dec_batch < COND_ROWS and d % V_DIM == 0 and dec_seq % GRID_W == 0

    cond = jnp.zeros((COND_ROWS, d), jnp.float32).at[:dec_batch].set(c).at[dec_batch].set(c_ctx)
    mods = _ada_mods(cond, ada_w, ada_b).reshape(depth, COND_ROWS, N_MOD, d)

    cast = lambda w: w.astype(MXU_DTYPE)
    hidden = ffn_w_out.shape[1]
    _, _, ni, nj = _ffn_grid(dec_batch * dec_seq, hidden)
    side_cast = depth > 1 and all(_cast_plan(w.shape, ni, nj) for w in (ffn_w_in, ffn_w_out))
    if side_cast:
        ffn_w = [(cast(ffn_w_in[:1]), cast(ffn_w_out[:1]), 0)]
    else:
        w_in_all, w_out_all = cast(ffn_w_in), cast(ffn_w_out)
        ffn_w = [(w_in_all, w_out_all, l) for l in range(depth)]
    cache_k = cache_k.reshape(cache_k.shape[:3] + (d,))
    cache_v = cache_v.reshape(cache_v.shape[:3] + (d,))
    streams = {
        "ctx": (x_prompt.reshape(batch * seq, d), seq, lambda i, tm: dec_batch),
        "lat": (x_sample.reshape(dec_batch * dec_seq, d), dec_seq,
                lambda i, tm: (i * tm) // dec_seq),
    }
    xs = {name: s[0] for name, s in streams.items()}
    pend = {name: None for name in streams}
    new_k, new_v = [], []
    for i in range(depth):
        kind, slot = i % N_MIXERS, i // N_MIXERS
        m_i = mods[i]
        gf = norm_ffn_g[i]
        last = i == depth - 1
        for name, (_, n, mod_row) in streams.items():
            x, pending = xs[name], pend[name]
            if kind == 0:
                x, h2 = _pool_mixer(x, pending, m_i, mod_row, norm_mix_g[i], gf, cast(pool_w[slot]),
                                    pool_scale[slot], n)
            elif kind == 1:
                x, h2 = _sgu_mixer(x, pending, m_i, mod_row, norm_mix_g[i], gf, cast(sgu_w_in[slot]),
                                   sgu_norm_g[slot], cast(sgu_ws[slot]), sgu_b[slot][:, :, None],
                                   cast(sgu_w_out[slot]))
            else:
                lam_init = 0.8 - 0.6 * math.exp(-0.3 * i)
                is_ctx = name == "ctx"
                outs = _qkv(x, pending, m_i, mod_row, norm_mix_g[i], cast(attn_w_qkv[slot]), n,
                            rope=not is_ctx, keep_f32=is_ctx)
                nb = x.shape[0] // n
                q, k, v = (a.reshape(nb, n, d) for a in outs[:3])
                if is_ctx:
                    new_k.append(outs[3])
                    new_v.append(outs[4])
                a = _attention(q, k, v, None if is_ctx else (cache_k, cache_v, slot),
                               attn_lambda[slot], attn_subln_g[slot], lam_init)
                x, h2 = _attn_out_proj(x, pending, a.reshape(-1, d), cast(attn_w_o[slot]), m_i, mod_row, gf)
            if last:
                xs[name] = _ffn(h2, *ffn_w[i], residual=(x, m_i, mod_row), final_g=final_g)
                pend[name] = None
            elif side_cast and i == 0 and name == "lat":
                f, w_in_rest, w_out_rest = _ffn(h2, *ffn_w[0], casts=(ffn_w_in, ffn_w_out))
                ffn_w += [(w_in_rest.reshape((depth - 1,) + ffn_w_in.shape[1:]),
                           w_out_rest.reshape((depth - 1,) + ffn_w_out.shape[1:]), l - 1)
                          for l in range(1, depth)]
                xs[name], pend[name] = x, (f, m_i)
            else:
                xs[name], pend[name] = x, (_ffn(h2, *ffn_w[i]), m_i)
    heads = d // V_DIM
    y_prompt = xs["ctx"].reshape(batch, seq, d)
    y_sample = xs["lat"].reshape(dec_batch, dec_seq, d)
    new_cache_k = jnp.stack([a.reshape(batch, seq, heads, 2, HEAD_DIM) for a in new_k], axis=1)
    new_cache_v = jnp.stack([a.reshape(batch, seq, heads, V_DIM) for a in new_v], axis=1)
    return (y_prompt, y_sample, new_cache_k, new_cache_v)
```

```python
import functools
import math

import jax
import jax.numpy as jnp
from jax import lax
from jax.experimental import pallas as pl
from jax.experimental.pallas import tpu as pltpu

EPS = 1e-6
N_MOD = 6
N_MIXERS = 3
POOL_WINDOWS = (2, 4, 8, 16)
POOL_HALO = 8
SGU_GROUPS = 8
SGU_CHUNK = 128
HEAD_DIM = 128
V_DIM = 2 * HEAD_DIM
GRID_W = 64
ROPE_BASE = 10000.0
ROPE_HALF = HEAD_DIM // 4
LOG2E = 1.4426950408889634
MXU_DTYPE = jnp.bfloat16
LANES = 128
PACKED_ROWS = 16
COND_ROWS = PACKED_ROWS
V7X_VMEM_BYTES = 64 * 1024 * 1024
VMEM_CAP_BYTES = V7X_VMEM_BYTES - 8 * 1024 * 1024
VMEM_FLOOR_BYTES = 16 * 1024 * 1024

ROW_TILE = 512
FFN_ROW_TILE = 1024
FFN_HIDDEN_TILE = 512
RESIDENT_WEIGHT_ROW_TILE = 256
ADA_COL_TILE = 1024
KV_CHUNK = 512
Q_TILE = 1024


def _vmem_limit(estimate_bytes):
    return int(min(max(estimate_bytes * 5 // 4, VMEM_FLOOR_BYTES), VMEM_CAP_BYTES))


def _params(semantics, vmem_estimate):
    return pltpu.CompilerParams(dimension_semantics=semantics,
                                vmem_limit_bytes=_vmem_limit(vmem_estimate))


def _dot(a, b):
    return jnp.dot(a, b, preferred_element_type=jnp.float32)


def _norm_mod(x, g, shift, scale):
    ms = jnp.mean(x * x, axis=-1, keepdims=True)
    return (x * lax.rsqrt(ms + EPS)) * (g * (1.0 + scale)) + shift


def _emit_ffn_input(h2_ref, x_new, gf_ref, mod_ref):
    h2_ref[...] = _norm_mod(x_new, gf_ref[...], mod_ref[3:4, :], mod_ref[4:5, :]).astype(h2_ref.dtype)


def _split_pending(refs, pending, n=1):
    xs, rest = refs[:n], refs[n:]
    if not pending:
        return xs, None, None, rest
    return xs, rest[:n], rest[n], rest[n + 1:]


def _stream_rows(x_ref, f_ref, modp_ref):
    x = x_ref[...]
    if f_ref is not None:
        x = x + modp_ref[5:6, :] * f_ref[...]
    return x


def _largest_tile(n, cap, quantum):
    t = min(cap, n)
    t -= t % quantum
    while n % t:
        t -= quantum
    return t


def _ada_kernel(cond_ref, w_ref, b_ref, o_ref):
    c = cond_ref[...]
    s = (c * jax.nn.sigmoid(c)).astype(MXU_DTYPE)
    o_ref[...] = _dot(s, w_ref[...].astype(MXU_DTYPE)) + b_ref[...]


def _ada_mods(cond, ada_w, ada_b):
    depth, d, n = ada_w.shape
    bn = _largest_tile(n, ADA_COL_TILE, LANES)
    est = 2 * d * bn * 4 + d * bn * 2 + 4 * COND_ROWS * (d + bn) * 4
    return pl.pallas_call(
        _ada_kernel,
        grid=(depth, n // bn),
        in_specs=[
            pl.BlockSpec((COND_ROWS, d), lambda l, j: (0, 0)),
            pl.BlockSpec((None, d, bn), lambda l, j: (l, 0, j)),
            pl.BlockSpec((None, 1, bn), lambda l, j: (l, 0, j)),
        ],
        out_specs=pl.BlockSpec((None, COND_ROWS, bn), lambda l, j: (l, 0, j)),
        out_shape=jax.ShapeDtypeStruct((depth, COND_ROWS, n), jnp.float32),
        compiler_params=_params(("parallel", "parallel"), est),
        name="ada_mods",
    )(cond, ada_w, ada_b.reshape(depth, 1, n))


def _ffn_kernel(*refs, n_hidden_blocks, residual, final_norm, n_casts):
    refs = list(refs)
    if residual:
        x_ref, mod_ref = refs[:2]
        refs = refs[2:]
    h_ref, wa_ref, wb_ref, wo_ref = refs[:4]
    refs = refs[4:]
    if final_norm:
        fg_ref = refs.pop(0)
    cast_src, refs = refs[:n_casts], refs[n_casts:]
    o_ref, cast_dst = refs[0], refs[1:]
    j = pl.program_id(1)

    @pl.when(j == 0)
    def _():
        o_ref[...] = jnp.zeros_like(o_ref)

    for src, dst in zip(cast_src, cast_dst):
        dst[...] = src[...].astype(dst.dtype)

    h = h_ref[...]
    a = _dot(h, wa_ref[...])
    b = _dot(h, wb_ref[...])
    act = (a * jax.nn.sigmoid(a)) * b
    o_ref[...] += _dot(act.astype(MXU_DTYPE), wo_ref[...])

    if residual:
        @pl.when(j == n_hidden_blocks - 1)
        def _():
            o_ref[...] = x_ref[...] + mod_ref[5:6, :] * o_ref[...]
            if final_norm:
                y = o_ref[...]
                ms = jnp.mean(y * y, axis=-1, keepdims=True)
                o_ref[...] = (o_ref[...] * lax.rsqrt(ms + EPS)) * fg_ref[...]


def _cast_plan(shape, ni, nj):
    layers, r, c = shape
    rest = (layers - 1) * r

    def rows_per_block(max_blocks):
        return next((br for br in range(PACKED_ROWS, r + 1, PACKED_ROWS)
                     if r % br == 0 and rest // br <= max_blocks), None)

    if c % nj == 0 and (c // nj) % LANES == 0 and rows_per_block(ni):
        br = rows_per_block(ni)
        last, first = rest // br - 1, r // br
        return (pl.BlockSpec((br, c // nj), lambda i, j: (first + jnp.minimum(i, last), j)),
                pl.BlockSpec((br, c // nj), lambda i, j: (jnp.minimum(i, last), j)), (rest, c))
    br = rows_per_block(ni * nj)
    if br is None:
        return None
    last, first = rest // br - 1, r // br
    return (pl.BlockSpec((br, c), lambda i, j: (first + jnp.minimum(i * nj + j, last), 0)),
            pl.BlockSpec((br, c), lambda i, j: (jnp.minimum(i * nj + j, last), 0)), (rest, c))


def _ffn_grid(t, f, fused_residual=False):
    tm = _largest_tile(t, ROW_TILE if fused_residual else FFN_ROW_TILE, PACKED_ROWS)
    th = _largest_tile(f, FFN_HIDDEN_TILE, LANES)
    return tm, th, t // tm, f // th


def _ffn(h, w_in, w_out, layer, residual=None, final_g=None, casts=()):
    t, d = h.shape
    f = w_out.shape[1]
    fused = residual is not None
    tm, th, ni, nj = _ffn_grid(t, f, fused)
    row_spec = pl.BlockSpec((tm, d), lambda i, j: (i, 0))
    in_specs, args = [], []
    if fused:
        x, mods, mod_row = residual
        in_specs += [row_spec, pl.BlockSpec((None, N_MOD, d), lambda i, j: (mod_row(i, tm), 0, 0))]
        args += [x, mods]
    in_specs += [
        row_spec,
        pl.BlockSpec((None, d, th), lambda i, j: (layer, 0, j)),
        pl.BlockSpec((None, d, th), lambda i, j: (layer, 0, j + nj)),
        pl.BlockSpec((None, th, d), lambda i, j: (layer, j, 0)),
    ]
    args += [h, w_in, w_in, w_out]
    if final_g is not None:
        in_specs.append(pl.BlockSpec((1, d), lambda i, j: (0, 0)))
        args.append(final_g.reshape(1, d))
    plans = [_cast_plan(a.shape, ni, nj) for a in casts]
    in_specs += [p[0] for p in plans]
    args += [a.reshape(-1, a.shape[-1]) for a in casts]
    est = (4 if fused else 2) * tm * d * 4 + 2 * tm * d * 2 + 6 * d * th * 2 + 7 * tm * th * 4 \
        + sum(2 * 6 * math.prod(p[0].block_shape) for p in plans)
    out = pl.pallas_call(
        functools.partial(_ffn_kernel, n_hidden_blocks=nj, residual=fused, final_norm=final_g is not None,
                          n_casts=len(casts)),
        grid=(ni, nj),
        in_specs=in_specs,
        out_specs=[row_spec] + [p[1] for p in plans],
        out_shape=[jax.ShapeDtypeStruct((t, d), jnp.float32)]
        + [jax.ShapeDtypeStruct(p[2], MXU_DTYPE) for p in plans],
        compiler_params=_params(("arbitrary" if casts else "parallel", "arbitrary"), est),
        name="ffn",
    )(*args)
    return out if casts else out[0]


def _pool_kernel(*refs, seq_len, blocks_per_seq, pending):
    (x_ref, xp_ref, xn_ref), fs, modp_ref, rest = _split_pending(list(refs), pending, 3)
    f_ref, fp_ref, fn_ref = fs if pending else (None, None, None)
    mod_ref, g_ref, gf_ref, w_ref, sc_ref, o_ref, h2_ref, h_sc, p_sc, d_sc = rest
    tm, d = x_ref.shape
    gd = d // len(POOL_WINDOWS)
    n = tm + 2 * POOL_HALO
    i = pl.program_id(0)
    blk = i % blocks_per_seq
    g = g_ref[...]
    shift = mod_ref[0:1, :]
    scale = mod_ref[1:2, :]
    x = _stream_rows(x_ref, f_ref, modp_ref)
    has_prev = (blk > 0).astype(jnp.float32)
    has_next = (blk < blocks_per_seq - 1).astype(jnp.float32)
    h_sc[0:POOL_HALO, :] = _norm_mod(_stream_rows(xp_ref, fp_ref, modp_ref), g, shift, scale) * has_prev
    h_sc[POOL_HALO:POOL_HALO + tm, :] = _norm_mod(x, g, shift, scale)
    h_sc[POOL_HALO + tm:n, :] = _norm_mod(_stream_rows(xn_ref, fn_ref, modp_ref), g, shift, scale) * has_next
    h_sc[n:, :] = jnp.zeros((POOL_HALO, d), jnp.float32)
    p_sc[n:, :] = jnp.zeros((POOL_HALO, gd), jnp.float32)

    pos = blk * tm + lax.broadcasted_iota(jnp.int32, (tm, 1), 0)
    for gi, win in enumerate(POOL_WINDOWS):
        cols = slice(gi * gd, (gi + 1) * gd)
        half = win // 2
        if half == 1:
            lo = h_sc[POOL_HALO - 1:POOL_HALO - 1 + tm, cols]
            hi = h_sc[POOL_HALO:POOL_HALO + tm, cols]
        else:
            p_sc[0:n, :] = h_sc[0:n, cols] + h_sc[1:n + 1, cols]
            s = 2
            while s < half:
                p_sc[0:n, :] = p_sc[0:n, :] + p_sc[s:n + s, :]
                s *= 2
            lo = p_sc[POOL_HALO - half:POOL_HALO - half + tm, :]
            hi = p_sc[POOL_HALO:POOL_HALO + tm, :]
        cnt = jnp.minimum(pos + half, seq_len) - jnp.maximum(pos - half, 0)
        inv_cnt = 1.0 / cnt.astype(jnp.float32)
        dev = (lo + hi) * inv_cnt - h_sc[POOL_HALO:POOL_HALO + tm, cols]
        d_sc[:, cols] = _dot(dev.astype(MXU_DTYPE), w_ref[gi])
    x_new = x + d_sc[...] * (mod_ref[2:3, :] * sc_ref[...])
    o_ref[...] = x_new
    _emit_ffn_input(h2_ref, x_new, gf_ref, mod_ref)


def _pool_mixer(x, pending, mods, mod_row, g, gf, w, scale, seq_len):
    t, d = x.shape
    tm = _largest_tile(seq_len, ROW_TILE, PACKED_ROWS)
    bps = seq_len // tm
    hb = tm // POOL_HALO
    n_halo_blocks = t // POOL_HALO
    gd = d // len(POOL_WINDOWS)
    row_spec = pl.BlockSpec((tm, d), lambda i: (i, 0))
    vec_spec = pl.BlockSpec((1, d), lambda i: (0, 0))
    mod_spec = pl.BlockSpec((None, N_MOD, d), lambda i: (mod_row(i, tm), 0, 0))
    halo_specs = [
        row_spec,
        pl.BlockSpec((POOL_HALO, d), lambda i: (jnp.maximum(i * hb - 1, 0), 0)),
        pl.BlockSpec((POOL_HALO, d), lambda i: (jnp.minimum((i + 1) * hb, n_halo_blocks - 1), 0)),
    ]
    in_specs, args = list(halo_specs), [x, x, x]
    if pending:
        f, mods_prev = pending
        in_specs += halo_specs + [mod_spec]
        args += [f, f, f, mods_prev]
    in_specs += [mod_spec, vec_spec, vec_spec, pl.BlockSpec(w.shape, lambda i: (0, 0, 0)), vec_spec]
    args += [mods, g.reshape(1, d), gf.reshape(1, d), w, scale.reshape(1, d)]
    est = (6 if pending else 4) * tm * d * 4 + 2 * tm * d * 2 + 2 * (tm + 3 * POOL_HALO) * d * 4 \
        + 2 * w.size * 2 + 6 * tm * gd * 4
    return pl.pallas_call(
        functools.partial(_pool_kernel, seq_len=seq_len, blocks_per_seq=bps, pending=bool(pending)),
        grid=(t // tm,),
        in_specs=in_specs,
        out_specs=[row_spec, row_spec],
        out_shape=[jax.ShapeDtypeStruct((t, d), jnp.float32), jax.ShapeDtypeStruct((t, d), MXU_DTYPE)],
        scratch_shapes=[pltpu.VMEM((tm + 3 * POOL_HALO, d), jnp.float32),
                        pltpu.VMEM((tm + 3 * POOL_HALO, gd), jnp.float32),
                        pltpu.VMEM((tm, d), jnp.float32)],
        compiler_params=_params(("parallel",), est),
        name="pool_mixer",
    )(*args)


def _sgu_kernel(*refs, pending):
    (x_ref,), fs, modp_ref, rest = _split_pending(list(refs), pending)
    mod_ref, g_ref, gf_ref, win_ref, ng_ref, ws_ref, bs_ref, wout_ref, o_ref, h2_ref, h_sc, v_sc, p_sc = rest
    tm, d = x_ref.shape
    sd = wout_ref.shape[0]
    gd = sd // SGU_GROUPS
    x = _stream_rows(x_ref, fs[0] if pending else None, modp_ref)
    h_sc[...] = _norm_mod(x, g_ref[...], mod_ref[0:1, :], mod_ref[1:2, :]).astype(MXU_DTYPE)
    h = h_sc[...]

    def gelu(z):
        return 0.5 * z * (1.0 + lax.erf(z * (1.0 / math.sqrt(2.0))))

    ssq = jnp.zeros((tm, 1), jnp.float32)
    for gi in range(SGU_GROUPS):
        v = gelu(_dot(h, win_ref[:, sd + gi * gd:sd + (gi + 1) * gd]))
        ssq = ssq + jnp.sum(v * v, axis=-1, keepdims=True)
        v_sc[:, gi * gd:(gi + 1) * gd] = v
    rstd = lax.rsqrt(ssq * (1.0 / sd) + EPS)

    for gi in range(SGU_GROUPS):
        cols = slice(gi * gd, (gi + 1) * gd)
        vn = (v_sc[:, cols] * rstd * ng_ref[:, cols]).astype(MXU_DTYPE)
        u = gelu(_dot(h, win_ref[:, cols]))
        wsg = ws_ref[gi]
        bias = bs_ref[gi]
        for c in range(tm // SGU_CHUNK):
            rows = slice(c * SGU_CHUNK, (c + 1) * SGU_CHUNK)
            vm = _dot(wsg, vn[rows, :]) + bias
            p_sc[rows, cols] = (u[rows, :] * vm).astype(MXU_DTYPE)
    x_new = x + mod_ref[2:3, :] * _dot(p_sc[...], wout_ref[...])
    o_ref[...] = x_new
    _emit_ffn_input(h2_ref, x_new, gf_ref, mod_ref)


def _sgu_mixer(x, pending, mods, mod_row, g, gf, w_in, norm_g, ws, bs, w_out):
    t, d = x.shape
    sd = w_out.shape[0]
    tm = _largest_tile(t, RESIDENT_WEIGHT_ROW_TILE, SGU_CHUNK)
    est = (w_in.size + w_out.size + ws.size) * 2 + (6 if pending else 4) * tm * d * 4 + tm * sd * 4 \
        + 4 * tm * d * 2 + 6 * tm * (sd // SGU_GROUPS) * 4
    const = dict(pipeline_mode=pl.Buffered(1))
    row_spec = pl.BlockSpec((tm, d), lambda i: (i, 0))
    vec_spec = pl.BlockSpec((1, d), lambda i: (0, 0))
    mod_spec = pl.BlockSpec((None, N_MOD, d), lambda i: (mod_row(i, tm), 0, 0))
    in_specs, args = [row_spec], [x]
    if pending:
        in_specs += [row_spec, mod_spec]
        args += list(pending)
    in_specs += [
        mod_spec,
        vec_spec,
        vec_spec,
        pl.BlockSpec(w_in.shape, lambda i: (0, 0), **const),
        pl.BlockSpec((1, sd), lambda i: (0, 0)),
        pl.BlockSpec(ws.shape, lambda i: (0, 0, 0), **const),
        pl.BlockSpec(bs.shape, lambda i: (0, 0, 0), **const),
        pl.BlockSpec(w_out.shape, lambda i: (0, 0), **const),
    ]
    args += [mods, g.reshape(1, d), gf.reshape(1, d), w_in, norm_g.reshape(1, sd), ws, bs, w_out]
    return pl.pallas_call(
        functools.partial(_sgu_kernel, pending=bool(pending)),
        grid=(t // tm,),
        in_specs=in_specs,
        out_specs=[row_spec, row_spec],
        out_shape=[jax.ShapeDtypeStruct((t, d), jnp.float32), jax.ShapeDtypeStruct((t, d), MXU_DTYPE)],
        scratch_shapes=[pltpu.VMEM((tm, d), MXU_DTYPE), pltpu.VMEM((tm, sd), jnp.float32),
                        pltpu.VMEM((tm, sd), MXU_DTYPE)],
        compiler_params=_params(("parallel",), est),
        name="sgu_mixer",
    )(*args)


def _rope_tables(n_tokens):
    rows = n_tokens // GRID_W
    row = jnp.broadcast_to(jnp.arange(rows, dtype=jnp.float32)[:, None], (rows, GRID_W)).reshape(-1)
    col = jnp.broadcast_to(jnp.arange(GRID_W, dtype=jnp.float32)[None, :], (rows, GRID_W)).reshape(-1)
    inv = ROPE_BASE ** (-jnp.arange(ROPE_HALF, dtype=jnp.float32) / ROPE_HALF)
    ang = jnp.stack([row[:, None] * inv, col[:, None] * inv], axis=1)
    cos, sin = jnp.cos(ang), jnp.sin(ang)
    c = jnp.stack([cos, cos], axis=2).reshape(n_tokens, HEAD_DIM)
    s = jnp.stack([-sin, sin], axis=2).reshape(n_tokens, HEAD_DIM)
    return c, s


def _qkv_kernel(*refs, pending, rope, keep_f32, q_scale):
    (x_ref,), fs, modp_ref, rest = _split_pending(list(refs), pending)
    mod_ref, g_ref, w_ref = rest[:3]
    rest = rest[3:]
    if rope:
        c_ref, s_ref = rest[:2]
        rest = rest[2:]
    q_ref, k_ref, v_ref = rest[:3]
    rest = rest[3:]
    if keep_f32:
        kf_ref, vf_ref = rest
    tm, d = x_ref.shape

    def rotate(dst_ref, y, mult):
        c = c_ref[...]
        s = s_ref[...]
        lane = lax.broadcasted_iota(jnp.int32, (tm, HEAD_DIM), 1)
        low_half = (lane % (2 * ROPE_HALF)) < ROPE_HALF
        for gi in range(d // HEAD_DIM):
            cols = slice(gi * HEAD_DIM, (gi + 1) * HEAD_DIM)
            yg = y[:, cols]
            partner = jnp.where(low_half,
                                pltpu.roll(yg, HEAD_DIM - ROPE_HALF, 1),
                                pltpu.roll(yg, ROPE_HALF, 1))
            r = yg * c + partner * s
            if mult != 1.0:
                r = r * mult
            dst_ref[:, cols] = r.astype(dst_ref.dtype)

    x = _stream_rows(x_ref, fs[0] if pending else None, modp_ref)
    h = _norm_mod(x, g_ref[...], mod_ref[0:1, :], mod_ref[1:2, :]).astype(MXU_DTYPE)

    y = _dot(h, w_ref[:, 0:d])
    if rope:
        rotate(q_ref, y, q_scale)
    else:
        q_ref[...] = (y * q_scale).astype(q_ref.dtype)

    y = _dot(h, w_ref[:, d:2 * d])
    if keep_f32:
        kf_ref[...] = y
    if rope:
        rotate(k_ref, y, 1.0)
    else:
        k_ref[...] = y.astype(k_ref.dtype)

    y = _dot(h, w_ref[:, 2 * d:3 * d])
    if keep_f32:
        vf_ref[...] = y
    v_ref[...] = y.astype(v_ref.dtype)


def _qkv(x, pending, mods, mod_row, g, w_qkv, seq_len, rope, keep_f32):
    t, d = x.shape
    tm = _largest_tile(seq_len, RESIDENT_WEIGHT_ROW_TILE, PACKED_ROWS)
    bps = seq_len // tm
    q_scale = HEAD_DIM ** -0.5 * LOG2E
    row_spec = pl.BlockSpec((tm, d), lambda i: (i, 0))
    mod_spec = pl.BlockSpec((None, N_MOD, d), lambda i: (mod_row(i, tm), 0, 0))
    in_specs, args = [row_spec], [x]
    if pending:
        in_specs += [row_spec, mod_spec]
        args += list(pending)
    in_specs += [mod_spec, pl.BlockSpec((1, d), lambda i: (0, 0)),
                 pl.BlockSpec(w_qkv.shape, lambda i: (0, 0), pipeline_mode=pl.Buffered(1))]
    args += [mods, g.reshape(1, d), w_qkv]
    if rope:
        c, s = _rope_tables(seq_len)
        in_specs += [pl.BlockSpec((tm, HEAD_DIM), lambda i: (i % bps, 0))] * 2
        args += [c, s]
    out_specs = [row_spec] * 3
    out_shape = [jax.ShapeDtypeStruct((t, d), MXU_DTYPE)] * 3
    if keep_f32:
        out_specs += [row_spec] * 2
        out_shape += [jax.ShapeDtypeStruct((t, d), jnp.float32)] * 2
    est = w_qkv.size * 2 + (4 if pending else 2) * tm * d * 4 + 7 * tm * d * 2 + 4 * tm * d * 4 \
        + (4 * tm * d * 4 if keep_f32 else 0)
    return pl.pallas_call(
        functools.partial(_qkv_kernel, pending=bool(pending), rope=rope, keep_f32=keep_f32, q_scale=q_scale),
        grid=(t // tm,),
        in_specs=in_specs,
        out_specs=out_specs,
        out_shape=out_shape,
        compiler_params=_params(("parallel",), est),
        name="attn_qkv",
    )(*args)


def _attn_kernel(q_ref, k_ref, v_ref, *rest, lam_init, has_cache):
    if has_cache:
        ck_ref, cv_ref, lp_ref, sg_ref, o_ref = rest
        cached_v = cv_ref[...].astype(MXU_DTYPE)
    else:
        lp_ref, sg_ref, o_ref = rest
    n_new = k_ref.shape[0]
    chunk = min(KV_CHUNK, n_new)
    lp = lp_ref[...]
    lam = (jnp.exp(jnp.sum(lp[0:1, :] * lp[1:2, :], axis=-1, keepdims=True))
           - jnp.exp(jnp.sum(lp[2:3, :] * lp[3:4, :], axis=-1, keepdims=True)) + lam_init)
    for hd in range(q_ref.shape[1] // V_DIM):
        vcols = slice(hd * V_DIM, (hd + 1) * V_DIM)
        outs = []
        for mp in range(2):
            cols = slice(hd * V_DIM + mp * HEAD_DIM, hd * V_DIM + (mp + 1) * HEAD_DIM)
            q = q_ref[:, cols]
            kv = []
            if has_cache:
                kv.append((ck_ref[:, cols].astype(MXU_DTYPE), cached_v[:, vcols]))
            for c0 in range(0, n_new, chunk):
                kv.append((k_ref[c0:c0 + chunk, cols], v_ref[c0:c0 + chunk, vcols]))
            m = denom = acc = None
            for kc, vc in kv:
                s = lax.dot_general(q, kc, (((1,), (1,)), ((), ())), preferred_element_type=jnp.float32)
                mc = jnp.max(s, axis=-1, keepdims=True)
                if m is None:
                    m = mc
                    p = jnp.exp2(s - m)
                    denom = jnp.sum(p, axis=-1, keepdims=True)
                    acc = _dot(p.astype(MXU_DTYPE), vc)
                else:
                    m_new = jnp.maximum(m, mc)
                    alpha = jnp.exp2(m - m_new)
                    p = jnp.exp2(s - m_new)
                    denom = alpha * denom + jnp.sum(p, axis=-1, keepdims=True)
                    acc = alpha * acc + _dot(p.astype(MXU_DTYPE), vc)
                    m = m_new
            outs.append(acc / denom)
        o = outs[0] - lam * outs[1]
        ms = jnp.mean(o * o, axis=-1, keepdims=True)
        o = (o * lax.rsqrt(ms + EPS) * sg_ref[...]) * (1.0 - lam_init)
        o_ref[:, vcols] = o.astype(o_ref.dtype)


def _attention(q, k, v, cache, lam_params, subln_g, lam_init):
    b, n, d = q.shape
    tq = _largest_tile(n, Q_TILE, PACKED_ROWS)
    past = 0 if cache is None else cache[0].shape[2]
    hw = d if n + past <= KV_CHUNK else V_DIM
    in_specs = [
        pl.BlockSpec((None, tq, hw), lambda bi, hi, qi: (bi, qi, hi)),
        pl.BlockSpec((None, n, hw), lambda bi, hi, qi: (bi, 0, hi)),
        pl.BlockSpec((None, n, hw), lambda bi, hi, qi: (bi, 0, hi)),
    ]
    args = [q, k, v]
    if cache is not None:
        cache_k, cache_v, slot = cache
        in_specs += [pl.BlockSpec((None, None, past, hw), lambda bi, hi, qi: (bi, slot, 0, hi))] * 2
        args += [cache_k, cache_v]
    in_specs += [
        pl.BlockSpec(lam_params.shape, lambda bi, hi, qi: (0, 0)),
        pl.BlockSpec((1, V_DIM), lambda bi, hi, qi: (0, 0)),
    ]
    args += [lam_params, subln_g.reshape(1, V_DIM)]
    chunk = min(KV_CHUNK, n)
    est = 4 * tq * hw * 2 + 4 * n * hw * 2 + 4 * past * hw * 4 \
        + (hw // V_DIM) * (6 * tq * chunk * 4 + 8 * tq * V_DIM * 4)
    return pl.pallas_call(
        functools.partial(_attn_kernel, lam_init=lam_init, has_cache=cache is not None),
        grid=(b, d // hw, n // tq),
        in_specs=in_specs,
        out_specs=pl.BlockSpec((None, tq, hw), lambda bi, hi, qi: (bi, qi, hi)),
        out_shape=jax.ShapeDtypeStruct((b, n, d), MXU_DTYPE),
        compiler_params=_params(("parallel", "parallel", "parallel"), est),
        name="diff_attention",
    )(*args)


def _proj_kernel(*refs, pending):
    (x_ref,), fs, modp_ref, rest = _split_pending(list(refs), pending)
    a_ref, w_ref, mod_ref, gf_ref, o_ref, h2_ref = rest
    half = x_ref.shape[0] // 2
    for r0 in (0, half):
        rows = slice(r0, r0 + half)
        x = _stream_rows(x_ref.at[rows], fs[0].at[rows] if pending else None, modp_ref)
        x_new = x + mod_ref[2:3, :] * _dot(a_ref[rows, :], w_ref[...])
        o_ref[rows, :] = x_new
        _emit_ffn_input(h2_ref.at[rows], x_new, gf_ref, mod_ref)


def _attn_out_proj(x, pending, a, w_o, mods, mod_row, gf):
    t, d = x.shape
    tm = _largest_tile(t, ROW_TILE, PACKED_ROWS)
    est = 4 * tm * d * 2 + 2 * d * d * 2 + (8 if pending else 6) * tm * d * 4
    row_spec = pl.BlockSpec((tm, d), lambda i: (i, 0))
    mod_spec = pl.BlockSpec((None, N_MOD, d), lambda i: (mod_row(i, tm), 0, 0))
    in_specs, args = [row_spec], [x]
    if pending:
        in_specs += [row_spec, mod_spec]
        args += list(pending)
    in_specs += [row_spec, pl.BlockSpec((d, d), lambda i: (0, 0)), mod_spec, pl.BlockSpec((1, d), lambda i: (0, 0))]
    args += [a, w_o, mods, gf.reshape(1, d)]
    return pl.pallas_call(
        functools.partial(_proj_kernel, pending=bool(pending)),
        grid=(t // tm,),
        in_specs=in_specs,
        out_specs=[row_spec, row_spec],
        out_shape=[jax.ShapeDtypeStruct((t, d), jnp.float32), jax.ShapeDtypeStruct((t, d), MXU_DTYPE)],
        compiler_params=_params(("parallel",), est),
        name="attn_out_proj",
    )(*args)


def kernel(x_prompt, x_sample, cache_k, cache_v, c, c_ctx, ada_w, ada_b, norm_mix_g, norm_ffn_g,
           pool_w, pool_scale, sgu_w_in, sgu_norm_g, sgu_ws, sgu_b, sgu_w_out,
           attn_w_qkv, attn_lambda, attn_subln_g, attn_w_o, ffn_w_in, ffn_w_out, final_g):
    batch, seq, d = x_prompt.shape
    dec_batch, dec_seq, _ = x_sample.shape
    depth = ada_w.shape[0]
    assert dec_batch < COND_ROWS and d % V_DIM == 0 and dec_seq % GRID_W == 0

    cond = jnp.zeros((COND_ROWS, d), jnp.float32).at[:dec_batch].set(c).at[dec_batch].set(c_ctx)
    mods = _ada_mods(cond, ada_w, ada_b).reshape(depth, COND_ROWS, N_MOD, d)

    cast = lambda w: w.astype(MXU_DTYPE)
    hidden = ffn_w_out.shape[1]
    _, _, ni, nj = _ffn_grid(dec_batch * dec_seq, hidden)
    side_cast = depth > 1 and all(_cast_plan(w.shape, ni, nj) for w in (ffn_w_in, ffn_w_out))
    if side_cast:
        ffn_w = [(cast(ffn_w_in[:1]), cast(ffn_w_out[:1]), 0)]
    else:
        w_in_all, w_out_all = cast(ffn_w_in), cast(ffn_w_out)
        ffn_w = [(w_in_all, w_out_all, l) for l in range(depth)]
    cache_k = cache_k.reshape(cache_k.shape[:3] + (d,))
    cache_v = cache_v.reshape(cache_v.shape[:3] + (d,))
    streams = {
        "ctx": (x_prompt.reshape(batch * seq, d), seq, lambda i, tm: dec_batch),
        "lat": (x_sample.reshape(dec_batch * dec_seq, d), dec_seq,
                lambda i, tm: (i * tm) // dec_seq),
    }
    xs = {name: s[0] for name, s in streams.items()}
    pend = {name: None for name in streams}
    new_k, new_v = [], []
    for i in range(depth):
        kind, slot = i % N_MIXERS, i // N_MIXERS
        m_i = mods[i]
        gf = norm_ffn_g[i]
        last = i == depth - 1
        for name, (_, n, mod_row) in streams.items():
            x, pending = xs[name], pend[name]
            if kind == 0:
                x, h2 = _pool_mixer(x, pending, m_i, mod_row, norm_mix_g[i], gf, cast(pool_w[slot]),
                                    pool_scale[slot], n)
            elif kind == 1:
                x, h2 = _sgu_mixer(x, pending, m_i, mod_row, norm_mix_g[i], gf, cast(sgu_w_in[slot]),
                                   sgu_norm_g[slot], cast(sgu_ws[slot]), sgu_b[slot][:, :, None],
                                   cast(sgu_w_out[slot]))
            else:
                lam_init = 0.8 - 0.6 * math.exp(-0.3 * i)
                is_ctx = name == "ctx"
                outs = _qkv(x, pending, m_i, mod_row, norm_mix_g[i], cast(attn_w_qkv[slot]), n,
                            rope=not is_ctx, keep_f32=is_ctx)
                nb = x.shape[0] // n
                q, k, v = (a.reshape(nb, n, d) for a in outs[:3])
                if is_ctx:
                    new_k.append(outs[3])
                    new_v.append(outs[4])
                a = _attention(q, k, v, None if is_ctx else (cache_k, cache_v, slot),
                               attn_lambda[slot], attn_subln_g[slot], lam_init)
                x, h2 = _attn_out_proj(x, pending, a.reshape(-1, d), cast(attn_w_o[slot]), m_i, mod_row, gf)
            if last:
                xs[name] = _ffn(h2, *ffn_w[i], residual=(x, m_i, mod_row), final_g=final_g)
                pend[name] = None
            elif side_cast and i == 0 and name == "lat":
                f, w_in_rest, w_out_rest = _ffn(h2, *ffn_w[0], casts=(ffn_w_in, ffn_w_out))
                ffn_w += [(w_in_rest.reshape((depth - 1,) + ffn_w_in.shape[1:]),
                           w_out_rest.reshape((depth - 1,) + ffn_w_out.shape[1:]), l - 1)
                          for l in range(1, depth)]
                xs[name], pend[name] = x, (f, m_i)
            else:
                xs[name], pend[name] = x, (_ffn(h2, *ffn_w[i]), m_i)
    heads = d // V_DIM
    y_prompt = xs["ctx"].reshape(batch, seq, d)
    y_sample = xs["lat"].reshape(dec_batch, dec_seq, d)
    new_cache_k = jnp.stack([a.reshape(batch, seq, heads, 2, HEAD_DIM) for a in new_k], axis=1)
    new_cache_v = jnp.stack([a.reshape(batch, seq, heads, V_DIM) for a in new_v], axis=1)
    return (y_prompt, y_sample, new_cache_k, new_cache_v)
```

```python
import functools
import math

import jax
import jax.numpy as jnp
from jax import lax
from jax.experimental import pallas as pl
from jax.experimental.pallas import tpu as pltpu

EPS = 1e-6
N_MOD = 6
N_MIXERS = 3
POOL_WINDOWS = (2, 4, 8, 16)
POOL_HALO = 8
SGU_GROUPS = 8
SGU_CHUNK = 128
HEAD_DIM = 128
V_DIM = 2 * HEAD_DIM
GRID_W = 64
ROPE_BASE = 10000.0
ROPE_HALF = HEAD_DIM // 4
LOG2E = 1.4426950408889634
MXU_DTYPE = jnp.bfloat16
LANES = 128
PACKED_ROWS = 16
COND_ROWS = PACKED_ROWS
V7X_VMEM_BYTES = 64 * 1024 * 1024
VMEM_CAP_BYTES = V7X_VMEM_BYTES - 8 * 1024 * 1024
VMEM_FLOOR_BYTES = 16 * 1024 * 1024

ROW_TILE = 512
FFN_ROW_TILE = 1024
FFN_HIDDEN_TILE = 512
RESIDENT_WEIGHT_ROW_TILE = 256
ADA_COL_TILE = 1024
KV_CHUNK = 512
Q_TILE = 1024


def _vmem_limit(estimate_bytes):
    return int(min(max(estimate_bytes * 5 // 4, VMEM_FLOOR_BYTES), VMEM_CAP_BYTES))


def _params(semantics, vmem_estimate):
    return pltpu.CompilerParams(dimension_semantics=semantics,
                                vmem_limit_bytes=_vmem_limit(vmem_estimate))


def _dot(a, b):
    return jnp.dot(a, b, preferred_element_type=jnp.float32)


def _norm_mod(x, g, shift, scale):
    ms = jnp.mean(x * x, axis=-1, keepdims=True)
    return (x * lax.rsqrt(ms + EPS)) * (g * (1.0 + scale)) + shift


def _emit_ffn_input(h2_ref, x_new, gf_ref, mod_ref):
    h2_ref[...] = _norm_mod(x_new, gf_ref[...], mod_ref[3:4, :], mod_ref[4:5, :]).astype(h2_ref.dtype)


def _split_pending(refs, pending, n=1):
    xs, rest = refs[:n], refs[n:]
    if not pending:
        return xs, None, None, rest
    return xs, rest[:n], rest[n], rest[n + 1:]


def _stream_rows(x_ref, f_ref, modp_ref):
    x = x_ref[...]
    if f_ref is not None:
        x = x + modp_ref[5:6, :] * f_ref[...]
    return x


def _largest_tile(n, cap, quantum):
    t = min(cap, n)
    t -= t % quantum
    while n % t:
        t -= quantum
    return t


def _ada_kernel(cond_ref, w_ref, b_ref, o_ref):
    c = cond_ref[...]
    s = (c * jax.nn.sigmoid(c)).astype(MXU_DTYPE)
    o_ref[...] = _dot(s, w_ref[...].astype(MXU_DTYPE)) + b_ref[...]


def _ada_mods(cond, ada_w, ada_b):
    depth, d, n = ada_w.shape
    bn = _largest_tile(n, ADA_COL_TILE, LANES)
    est = 2 * d * bn * 4 + d * bn * 2 + 4 * COND_ROWS * (d + bn) * 4
    return pl.pallas_call(
        _ada_kernel,
        grid=(depth, n // bn),
        in_specs=[
            pl.BlockSpec((COND_ROWS, d), lambda l, j: (0, 0)),
            pl.BlockSpec((None, d, bn), lambda l, j: (l, 0, j)),
            pl.BlockSpec((None, 1, bn), lambda l, j: (l, 0, j)),
        ],
        out_specs=pl.BlockSpec((None, COND_ROWS, bn), lambda l, j: (l, 0, j)),
        out_shape=jax.ShapeDtypeStruct((depth, COND_ROWS, n), jnp.float32),
        compiler_params=_params(("parallel", "parallel"), est),
        name="ada_mods",
    )(cond, ada_w, ada_b.reshape(depth, 1, n))


def _ffn_kernel(*refs, n_hidden_blocks, residual, final_norm, n_casts):
    refs = list(refs)
    if residual:
        x_ref, mod_ref = refs[:2]
        refs = refs[2:]
    h_ref, wa_ref, wb_ref, wo_ref = refs[:4]
    refs = refs[4:]
    if final_norm:
        fg_ref = refs.pop(0)
    cast_src, refs = refs[:n_casts], refs[n_casts:]
    o_ref, cast_dst = refs[0], refs[1:]
    j = pl.program_id(1)

    @pl.when(j == 0)
    def _():
        o_ref[...] = jnp.zeros_like(o_ref)

    for src, dst in zip(cast_src, cast_dst):
        dst[...] = src[...].astype(dst.dtype)

    h = h_ref[...]
    a = _dot(h, wa_ref[...])
    b = _dot(h, wb_ref[...])
    act = (a * jax.nn.sigmoid(a)) * b
    o_ref[...] += _dot(act.astype(MXU_DTYPE), wo_ref[...])

    if residual:
        @pl.when(j == n_hidden_blocks - 1)
        def _():
            o_ref[...] = x_ref[...] + mod_ref[5:6, :] * o_ref[...]
            if final_norm:
                y = o_ref[...]
                ms = jnp.mean(y * y, axis=-1, keepdims=True)
                o_ref[...] = (o_ref[...] * lax.rsqrt(ms + EPS)) * fg_ref[...]


def _cast_plan(shape, ni, nj):
    layers, r, c = shape
    rest = (layers - 1) * r

    def rows_per_block(max_blocks):
        return next((br for br in range(PACKED_ROWS, r + 1, PACKED_ROWS)
                     if r % br == 0 and rest // br <= max_blocks), None)

    if c % nj == 0 and (c // nj) % LANES == 0 and rows_per_block(ni):
        br = rows_per_block(ni)
        last, first = rest // br - 1, r // br
        return (pl.BlockSpec((br, c // nj), lambda i, j: (first + jnp.minimum(i, last), j)),
                pl.BlockSpec((br, c // nj), lambda i, j: (jnp.minimum(i, last), j)), (rest, c))
    br = rows_per_block(ni * nj)
    if br is None:
        return None
    last, first = rest // br - 1, r // br
    return (pl.BlockSpec((br, c), lambda i, j: (first + jnp.minimum(i * nj + j, last), 0)),
            pl.BlockSpec((br, c), lambda i, j: (jnp.minimum(i * nj + j, last), 0)), (rest, c))


def _ffn_grid(t, f, fused_residual=False):
    tm = _largest_tile(t, ROW_TILE if fused_residual else FFN_ROW_TILE, PACKED_ROWS)
    th = _largest_tile(f, FFN_HIDDEN_TILE, LANES)
    return tm, th, t // tm, f // th


def _ffn(h, w_in, w_out, layer, residual=None, final_g=None, casts=()):
    t, d = h.shape
    f = w_out.shape[1]
    fused = residual is not None
    tm, th, ni, nj = _ffn_grid(t, f, fused)
    row_spec = pl.BlockSpec((tm, d), lambda i, j: (i, 0))
    in_specs, args = [], []
    if fused:
        x, mods, mod_row = residual
        in_specs += [row_spec, pl.BlockSpec((None, N_MOD, d), lambda i, j: (mod_row(i, tm), 0, 0))]
        args += [x, mods]
    in_specs += [
        row_spec,
        pl.BlockSpec((None, d, th), lambda i, j: (layer, 0, j)),
        pl.BlockSpec((None, d, th), lambda i, j: (layer, 0, j + nj)),
        pl.BlockSpec((None, th, d), lambda i, j: (layer, j, 0)),
    ]
    args += [h, w_in, w_in, w_out]
    if final_g is not None:
        in_specs.append(pl.BlockSpec((1, d), lambda i, j: (0, 0)))
        args.append(final_g.reshape(1, d))
    plans = [_cast_plan(a.shape, ni, nj) for a in casts]
    in_specs += [p[0] for p in plans]
    args += [a.reshape(-1, a.shape[-1]) for a in casts]
    est = (4 if fused else 2) * tm * d * 4 + 2 * tm * d * 2 + 6 * d * th * 2 + 7 * tm * th * 4 \
        + sum(2 * 6 * math.prod(p[0].block_shape) for p in plans)
    out = pl.pallas_call(
        functools.partial(_ffn_kernel, n_hidden_blocks=nj, residual=fused, final_norm=final_g is not None,
                          n_casts=len(casts)),
        grid=(ni, nj),
        in_specs=in_specs,
        out_specs=[row_spec] + [p[1] for p in plans],
        out_shape=[jax.ShapeDtypeStruct((t, d), jnp.float32)]
        + [jax.ShapeDtypeStruct(p[2], MXU_DTYPE) for p in plans],
        compiler_params=_params(("arbitrary" if casts else "parallel", "arbitrary"), est),
        name="ffn",
    )(*args)
    return out if casts else out[0]


def _pool_kernel(*refs, seq_len, blocks_per_seq, pending):
    (x_ref, xp_ref, xn_ref), fs, modp_ref, rest = _split_pending(list(refs), pending, 3)
    f_ref, fp_ref, fn_ref = fs if pending else (None, None, None)
    mod_ref, g_ref, gf_ref, w_ref, sc_ref, o_ref, h2_ref, h_sc, p_sc, d_sc = rest
    tm, d = x_ref.shape
    gd = d // len(POOL_WINDOWS)
    n = tm + 2 * POOL_HALO
    i = pl.program_id(0)
    blk = i % blocks_per_seq
    g = g_ref[...]
    shift = mod_ref[0:1, :]
    scale = mod_ref[1:2, :]
    x = _stream_rows(x_ref, f_ref, modp_ref)
    has_prev = (blk > 0).astype(jnp.float32)
    has_next = (blk < blocks_per_seq - 1).astype(jnp.float32)
    h_sc[0:POOL_HALO, :] = _norm_mod(_stream_rows(xp_ref, fp_ref, modp_ref), g, shift, scale) * has_prev
    h_sc[POOL_HALO:POOL_HALO + tm, :] = _norm_mod(x, g, shift, scale)
    h_sc[POOL_HALO + tm:n, :] = _norm_mod(_stream_rows(xn_ref, fn_ref, modp_ref), g, shift, scale) * has_next
    h_sc[n:, :] = jnp.zeros((POOL_HALO, d), jnp.float32)
    p_sc[n:, :] = jnp.zeros((POOL_HALO, gd), jnp.float32)

    pos = blk * tm + lax.broadcasted_iota(jnp.int32, (tm, 1), 0)
    for gi, win in enumerate(POOL_WINDOWS):
        cols = slice(gi * gd, (gi + 1) * gd)
        half = win // 2
        if half == 1:
            lo = h_sc[POOL_HALO - 1:POOL_HALO - 1 + tm, cols]
            hi = h_sc[POOL_HALO:POOL_HALO + tm, cols]
        else:
            p_sc[0:n, :] = h_sc[0:n, cols] + h_sc[1:n + 1, cols]
            s = 2
            while s < half:
                p_sc[0:n, :] = p_sc[0:n, :] + p_sc[s:n + s, :]
                s *= 2
            lo = p_sc[POOL_HALO - half:POOL_HALO - half + tm, :]
            hi = p_sc[POOL_HALO:POOL_HALO + tm, :]
        cnt = jnp.minimum(pos + half, seq_len) - jnp.maximum(pos - half, 0)
        inv_cnt = 1.0 / cnt.astype(jnp.float32)
        dev = (lo + hi) * inv_cnt - h_sc[POOL_HALO:POOL_HALO + tm, cols]
        d_sc[:, cols] = _dot(dev.astype(MXU_DTYPE), w_ref[gi])
    x_new = x + d_sc[...] * (mod_ref[2:3, :] * sc_ref[...])
    o_ref[...] = x_new
    _emit_ffn_input(h2_ref, x_new, gf_ref, mod_ref)


def _pool_mixer(x, pending, mods, mod_row, g, gf, w, scale, seq_len):
    t, d = x.shape
    tm = _largest_tile(seq_len, ROW_TILE, PACKED_ROWS)
    bps = seq_len // tm
    hb = tm // POOL_HALO
    n_halo_blocks = t // POOL_HALO
    gd = d // len(POOL_WINDOWS)
    row_spec = pl.BlockSpec((tm, d), lambda i: (i, 0))
    vec_spec = pl.BlockSpec((1, d), lambda i: (0, 0))
    mod_spec = pl.BlockSpec((None, N_MOD, d), lambda i: (mod_row(i, tm), 0, 0))
    halo_specs = [
        row_spec,
        pl.BlockSpec((POOL_HALO, d), lambda i: (jnp.maximum(i * hb - 1, 0), 0)),
        pl.BlockSpec((POOL_HALO, d), lambda i: (jnp.minimum((i + 1) * hb, n_halo_blocks - 1), 0)),
    ]
    in_specs, args = list(halo_specs), [x, x, x]
    if pending:
        f, mods_prev = pending
        in_specs += halo_specs + [mod_spec]
        args += [f, f, f, mods_prev]
    in_specs += [mod_spec, vec_spec, vec_spec, pl.BlockSpec(w.shape, lambda i: (0, 0, 0)), vec_spec]
    args += [mods, g.reshape(1, d), gf.reshape(1, d), w, scale.reshape(1, d)]
    est = (6 if pending else 4) * tm * d * 4 + 2 * tm * d * 2 + 2 * (tm + 3 * POOL_HALO) * d * 4 \
        + 2 * w.size * 2 + 6 * tm * gd * 4
    return pl.pallas_call(
        functools.partial(_pool_kernel, seq_len=seq_len, blocks_per_seq=bps, pending=bool(pending)),
        grid=(t // tm,),
        in_specs=in_specs,
        out_specs=[row_spec, row_spec],
        out_shape=[jax.ShapeDtypeStruct((t, d), jnp.float32), jax.ShapeDtypeStruct((t, d), MXU_DTYPE)],
        scratch_shapes=[pltpu.VMEM((tm + 3 * POOL_HALO, d), jnp.float32),
                        pltpu.VMEM((tm + 3 * POOL_HALO, gd), jnp.float32),
                        pltpu.VMEM((tm, d), jnp.float32)],
        compiler_params=_params(("parallel",), est),
        name="pool_mixer",
    )(*args)


def _sgu_kernel(*refs, pending):
    (x_ref,), fs, modp_ref, rest = _split_pending(list(refs), pending)
    mod_ref, g_ref, gf_ref, win_ref, ng_ref, ws_ref, bs_ref, wout_ref, o_ref, h2_ref, h_sc, v_sc, p_sc = rest
    tm, d = x_ref.shape
    sd = wout_ref.shape[0]
    gd = sd // SGU_GROUPS
    x = _stream_rows(x_ref, fs[0] if pending else None, modp_ref)
    h_sc[...] = _norm_mod(x, g_ref[...], mod_ref[0:1, :], mod_ref[1:2, :]).astype(MXU_DTYPE)
    h = h_sc[...]

    def gelu(z):
        return 0.5 * z * (1.0 + lax.erf(z * (1.0 / math.sqrt(2.0))))

    ssq = jnp.zeros((tm, 1), jnp.float32)
    for gi in range(SGU_GROUPS):
        v = gelu(_dot(h, win_ref[:, sd + gi * gd:sd + (gi + 1) * gd]))
        ssq = ssq + jnp.sum(v * v, axis=-1, keepdims=True)
        v_sc[:, gi * gd:(gi + 1) * gd] = v
    rstd = lax.rsqrt(ssq * (1.0 / sd) + EPS)

    for gi in range(SGU_GROUPS):
        cols = slice(gi * gd, (gi + 1) * gd)
        vn = (v_sc[:, cols] * rstd * ng_ref[:, cols]).astype(MXU_DTYPE)
        u = gelu(_dot(h, win_ref[:, cols]))
        wsg = ws_ref[gi]
        bias = bs_ref[gi]
        for c in range(tm // SGU_CHUNK):
            rows = slice(c * SGU_CHUNK, (c + 1) * SGU_CHUNK)
            vm = _dot(wsg, vn[rows, :]) + bias
            p_sc[rows, cols] = (u[rows, :] * vm).astype(MXU_DTYPE)
    x_new = x + mod_ref[2:3, :] * _dot(p_sc[...], wout_ref[...])
    o_ref[...] = x_new
    _emit_ffn_input(h2_ref, x_new, gf_ref, mod_ref)


def _sgu_mixer(x, pending, mods, mod_row, g, gf, w_in, norm_g, ws, bs, w_out):
    t, d = x.shape
    sd = w_out.shape[0]
    tm = _largest_tile(t, RESIDENT_WEIGHT_ROW_TILE, SGU_CHUNK)
    est = (w_in.size + w_out.size + ws.size) * 2 + (6 if pending else 4) * tm * d * 4 + tm * sd * 4 \
        + 4 * tm * d * 2 + 6 * tm * (sd // SGU_GROUPS) * 4
    const = dict(pipeline_mode=pl.Buffered(1))
    row_spec = pl.BlockSpec((tm, d), lambda i: (i, 0))
    vec_spec = pl.BlockSpec((1, d), lambda i: (0, 0))
    mod_spec = pl.BlockSpec((None, N_MOD, d), lambda i: (mod_row(i, tm), 0, 0))
    in_specs, args = [row_spec], [x]
    if pending:
        in_specs += [row_spec, mod_spec]
        args += list(pending)
    in_specs += [
        mod_spec,
        vec_spec,
        vec_spec,
        pl.BlockSpec(w_in.shape, lambda i: (0, 0), **const),
        pl.BlockSpec((1, sd), lambda i: (0, 0)),
        pl.BlockSpec(ws.shape, lambda i: (0, 0, 0), **const),
        pl.BlockSpec(bs.shape, lambda i: (0, 0, 0), **const),
        pl.BlockSpec(w_out.shape, lambda i: (0, 0), **const),
    ]
    args += [mods, g.reshape(1, d), gf.reshape(1, d), w_in, norm_g.reshape(1, sd), ws, bs, w_out]
    return pl.pallas_call(
        functools.partial(_sgu_kernel, pending=bool(pending)),
        grid=(t // tm,),
        in_specs=in_specs,
        out_specs=[row_spec, row_spec],
        out_shape=[jax.ShapeDtypeStruct((t, d), jnp.float32), jax.ShapeDtypeStruct((t, d), MXU_DTYPE)],
        scratch_shapes=[pltpu.VMEM((tm, d), MXU_DTYPE), pltpu.VMEM((tm, sd), jnp.float32),
                        pltpu.VMEM((tm, sd), MXU_DTYPE)],
        compiler_params=_params(("parallel",), est),
        name="sgu_mixer",
    )(*args)


def _rope_tables(n_tokens):
    rows = n_tokens // GRID_W
    row = jnp.broadcast_to(jnp.arange(rows, dtype=jnp.float32)[:, None], (rows, GRID_W)).reshape(-1)
    col = jnp.broadcast_to(jnp.arange(GRID_W, dtype=jnp.float32)[None, :], (rows, GRID_W)).reshape(-1)
    inv = ROPE_BASE ** (-jnp.arange(ROPE_HALF, dtype=jnp.float32) / ROPE_HALF)
    ang = jnp.stack([row[:, None] * inv, col[:, None] * inv], axis=1)
    cos, sin = jnp.cos(ang), jnp.sin(ang)
    c = jnp.stack([cos, cos], axis=2).reshape(n_tokens, HEAD_DIM)
    s = jnp.stack([-sin, sin], axis=2).reshape(n_tokens, HEAD_DIM)
    return c, s


def _qkv_kernel(*refs, pending, rope, keep_f32, q_scale):
    (x_ref,), fs, modp_ref, rest = _split_pending(list(refs), pending)
    mod_ref, g_ref, w_ref = rest[:3]
    rest = rest[3:]
    if rope:
        c_ref, s_ref = rest[:2]
        rest = rest[2:]
    q_ref, k_ref, v_ref = rest[:3]
    rest = rest[3:]
    if keep_f32:
        kf_ref, vf_ref = rest
    tm, d = x_ref.shape

    def rotate(dst_ref, y, mult):
        c = c_ref[...]
        s = s_ref[...]
        lane = lax.broadcasted_iota(jnp.int32, (tm, HEAD_DIM), 1)
        low_half = (lane % (2 * ROPE_HALF)) < ROPE_HALF
        for gi in range(d // HEAD_DIM):
            cols = slice(gi * HEAD_DIM, (gi + 1) * HEAD_DIM)
            yg = y[:, cols]
            partner = jnp.where(low_half,
                                pltpu.roll(yg, HEAD_DIM - ROPE_HALF, 1),
                                pltpu.roll(yg, ROPE_HALF, 1))
            r = yg * c + partner * s
            if mult != 1.0:
                r = r * mult
            dst_ref[:, cols] = r.astype(dst_ref.dtype)

    x = _stream_rows(x_ref, fs[0] if pending else None, modp_ref)
    h = _norm_mod(x, g_ref[...], mod_ref[0:1, :], mod_ref[1:2, :]).astype(MXU_DTYPE)

    y = _dot(h, w_ref[:, 0:d])
    if rope:
        rotate(q_ref, y, q_scale)
    else:
        q_ref[...] = (y * q_scale).astype(q_ref.dtype)

    y = _dot(h, w_ref[:, d:2 * d])
    if keep_f32:
        for gi in range(d // HEAD_DIM):
            kf_ref[:, gi, :] = y[:, gi * HEAD_DIM:(gi + 1) * HEAD_DIM]
    if rope:
        rotate(k_ref, y, 1.0)
    else:
        k_ref[...] = y.astype(k_ref.dtype)

    y = _dot(h, w_ref[:, 2 * d:3 * d])
    if keep_f32:
        vf_ref[...] = y
    v_ref[...] = y.astype(v_ref.dtype)


def _qkv(x, pending, mods, mod_row, g, w_qkv, seq_len, rope, keep_f32):
    t, d = x.shape
    tm = _largest_tile(seq_len, RESIDENT_WEIGHT_ROW_TILE, PACKED_ROWS)
    bps = seq_len // tm
    q_scale = HEAD_DIM ** -0.5 * LOG2E
    row_spec = pl.BlockSpec((tm, d), lambda i: (i, 0))
    mod_spec = pl.BlockSpec((None, N_MOD, d), lambda i: (mod_row(i, tm), 0, 0))
    in_specs, args = [row_spec], [x]
    if pending:
        in_specs += [row_spec, mod_spec]
        args += list(pending)
    in_specs += [mod_spec, pl.BlockSpec((1, d), lambda i: (0, 0)),
                 pl.BlockSpec(w_qkv.shape, lambda i: (0, 0), pipeline_mode=pl.Buffered(1))]
    args += [mods, g.reshape(1, d), w_qkv]
    if rope:
        c, s = _rope_tables(seq_len)
        in_specs += [pl.BlockSpec((tm, HEAD_DIM), lambda i: (i % bps, 0))] * 2
        args += [c, s]
    out_specs = [row_spec] * 3
    out_shape = [jax.ShapeDtypeStruct((t, d), MXU_DTYPE)] * 3
    if keep_f32:
        out_specs += [pl.BlockSpec((tm, d // HEAD_DIM, HEAD_DIM), lambda i: (i, 0, 0)), row_spec]
        out_shape += [jax.ShapeDtypeStruct((t, d // HEAD_DIM, HEAD_DIM), jnp.float32),
                      jax.ShapeDtypeStruct((t, d), jnp.float32)]
    est = w_qkv.size * 2 + (4 if pending else 2) * tm * d * 4 + 7 * tm * d * 2 + 4 * tm * d * 4 \
        + (4 * tm * d * 4 if keep_f32 else 0)
    return pl.pallas_call(
        functools.partial(_qkv_kernel, pending=bool(pending), rope=rope, keep_f32=keep_f32, q_scale=q_scale),
        grid=(t // tm,),
        in_specs=in_specs,
        out_specs=out_specs,
        out_shape=out_shape,
        compiler_params=_params(("parallel",), est),
        name="attn_qkv",
    )(*args)


def _attn_kernel(q_ref, k_ref, v_ref, *rest, lam_init, has_cache):
    if has_cache:
        ck_ref, cv_ref, lp_ref, sg_ref, o_ref = rest
        cached_v = cv_ref[...].astype(MXU_DTYPE)
    else:
        lp_ref, sg_ref, o_ref = rest
    n_new = k_ref.shape[0]
    chunk = min(KV_CHUNK, n_new)
    lp = lp_ref[...]
    lam = (jnp.exp(jnp.sum(lp[0:1, :] * lp[1:2, :], axis=-1, keepdims=True))
           - jnp.exp(jnp.sum(lp[2:3, :] * lp[3:4, :], axis=-1, keepdims=True)) + lam_init)
    for hd in range(q_ref.shape[1] // V_DIM):
        vcols = slice(hd * V_DIM, (hd + 1) * V_DIM)
        outs = []
        for mp in range(2):
            cols = slice(hd * V_DIM + mp * HEAD_DIM, hd * V_DIM + (mp + 1) * HEAD_DIM)
            q = q_ref[:, cols]
            kv = []
            if has_cache:
                kv.append((ck_ref[:, cols].astype(MXU_DTYPE), cached_v[:, vcols]))
            for c0 in range(0, n_new, chunk):
                kv.append((k_ref[c0:c0 + chunk, cols], v_ref[c0:c0 + chunk, vcols]))
            m = denom = acc = None
            for kc, vc in kv:
                s = lax.dot_general(q, kc, (((1,), (1,)), ((), ())), preferred_element_type=jnp.float32)
                mc = jnp.max(s, axis=-1, keepdims=True)
                if m is None:
                    m = mc
                    p = jnp.exp2(s - m)
                    denom = jnp.sum(p, axis=-1, keepdims=True)
                    acc = _dot(p.astype(MXU_DTYPE), vc)
                else:
                    m_new = jnp.maximum(m, mc)
                    alpha = jnp.exp2(m - m_new)
                    p = jnp.exp2(s - m_new)
                    denom = alpha * denom + jnp.sum(p, axis=-1, keepdims=True)
                    acc = alpha * acc + _dot(p.astype(MXU_DTYPE), vc)
                    m = m_new
            outs.append(acc / denom)
        o = outs[0] - lam * outs[1]
        ms = jnp.mean(o * o, axis=-1, keepdims=True)
        o = (o * lax.rsqrt(ms + EPS) * sg_ref[...]) * (1.0 - lam_init)
        o_ref[:, vcols] = o.astype(o_ref.dtype)


def _attention(q, k, v, cache, lam_params, subln_g, lam_init):
    b, n, d = q.shape
    tq = _largest_tile(n, Q_TILE, PACKED_ROWS)
    past = 0 if cache is None else cache[0].shape[2]
    hw = d if n + past <= KV_CHUNK else V_DIM
    in_specs = [
        pl.BlockSpec((None, tq, hw), lambda bi, hi, qi: (bi, qi, hi)),
        pl.BlockSpec((None, n, hw), lambda bi, hi, qi: (bi, 0, hi)),
        pl.BlockSpec((None, n, hw), lambda bi, hi, qi: (bi, 0, hi)),
    ]
    args = [q, k, v]
    if cache is not None:
        cache_k, cache_v, slot = cache
        in_specs += [pl.BlockSpec((None, None, past, hw), lambda bi, hi, qi: (bi, slot, 0, hi))] * 2
        args += [cache_k, cache_v]
    in_specs += [
        pl.BlockSpec(lam_params.shape, lambda bi, hi, qi: (0, 0)),
        pl.BlockSpec((1, V_DIM), lambda bi, hi, qi: (0, 0)),
    ]
    args += [lam_params, subln_g.reshape(1, V_DIM)]
    chunk = min(KV_CHUNK, n)
    est = 4 * tq * hw * 2 + 4 * n * hw * 2 + 4 * past * hw * 4 \
        + (hw // V_DIM) * (6 * tq * chunk * 4 + 8 * tq * V_DIM * 4)
    return pl.pallas_call(
        functools.partial(_attn_kernel, lam_init=lam_init, has_cache=cache is not None),
        grid=(b, d // hw, n // tq),
        in_specs=in_specs,
        out_specs=pl.BlockSpec((None, tq, hw), lambda bi, hi, qi: (bi, qi, hi)),
        out_shape=jax.ShapeDtypeStruct((b, n, d), MXU_DTYPE),
        compiler_params=_params(("parallel", "parallel", "parallel"), est),
        name="diff_attention",
    )(*args)


def _proj_kernel(*refs, pending):
    (x_ref,), fs, modp_ref, rest = _split_pending(list(refs), pending)
    a_ref, w_ref, mod_ref, gf_ref, o_ref, h2_ref = rest
    half = x_ref.shape[0] // 2
    for r0 in (0, half):
        rows = slice(r0, r0 + half)
        x = _stream_rows(x_ref.at[rows], fs[0].at[rows] if pending else None, modp_ref)
        x_new = x + mod_ref[2:3, :] * _dot(a_ref[rows, :], w_ref[...])
        o_ref[rows, :] = x_new
        _emit_ffn_input(h2_ref.at[rows], x_new, gf_ref, mod_ref)


def _attn_out_proj(x, pending, a, w_o, mods, mod_row, gf):
    t, d = x.shape
    tm = _largest_tile(t, ROW_TILE, PACKED_ROWS)
    est = 4 * tm * d * 2 + 2 * d * d * 2 + (8 if pending else 6) * tm * d * 4
    row_spec = pl.BlockSpec((tm, d), lambda i: (i, 0))
    mod_spec = pl.BlockSpec((None, N_MOD, d), lambda i: (mod_row(i, tm), 0, 0))
    in_specs, args = [row_spec], [x]
    if pending:
        in_specs += [row_spec, mod_spec]
        args += list(pending)
    in_specs += [row_spec, pl.BlockSpec((d, d), lambda i: (0, 0)), mod_spec, pl.BlockSpec((1, d), lambda i: (0, 0))]
    args += [a, w_o, mods, gf.reshape(1, d)]
    return pl.pallas_call(
        functools.partial(_proj_kernel, pending=bool(pending)),
        grid=(t // tm,),
        in_specs=in_specs,
        out_specs=[row_spec, row_spec],
        out_shape=[jax.ShapeDtypeStruct((t, d), jnp.float32), jax.ShapeDtypeStruct((t, d), MXU_DTYPE)],
        compiler_params=_params(("parallel",), est),
        name="attn_out_proj",
    )(*args)


def kernel(x_prompt, x_sample, cache_k, cache_v, c, c_ctx, ada_w, ada_b, norm_mix_g, norm_ffn_g,
           pool_w, pool_scale, sgu_w_in, sgu_norm_g, sgu_ws, sgu_b, sgu_w_out,
           attn_w_qkv, attn_lambda, attn_subln_g, attn_w_o, ffn_w_in, ffn_w_out, final_g):
    batch, seq, d = x_prompt.shape
    dec_batch, dec_seq, _ = x_sample.shape
    depth = ada_w.shape[0]
    assert dec_batch < COND_ROWS and d % V_DIM == 0 and dec_seq % GRID_W == 0

    cond = jnp.zeros((COND_ROWS, d), jnp.float32).at[:dec_batch].set(c).at[dec_batch].set(c_ctx)
    mods = _ada_mods(cond, ada_w, ada_b).reshape(depth, COND_ROWS, N_MOD, d)

    cast = lambda w: w.astype(MXU_DTYPE)
    hidden = ffn_w_out.shape[1]
    _, _, ni, nj = _ffn_grid(dec_batch * dec_seq, hidden)
    side_cast = depth > 1 and all(_cast_plan(w.shape, ni, nj) for w in (ffn_w_in, ffn_w_out))
    if side_cast:
        ffn_w = [(cast(ffn_w_in[:1]), cast(ffn_w_out[:1]), 0)]
    else:
        w_in_all, w_out_all = cast(ffn_w_in), cast(ffn_w_out)
        ffn_w = [(w_in_all, w_out_all, l) for l in range(depth)]
    cache_k = cache_k.reshape(cache_k.shape[:3] + (d,))
    cache_v = cache_v.reshape(cache_v.shape[:3] + (d,))
    streams = {
        "ctx": (x_prompt.reshape(batch * seq, d), seq, lambda i, tm: dec_batch),
        "lat": (x_sample.reshape(dec_batch * dec_seq, d), dec_seq,
                lambda i, tm: (i * tm) // dec_seq),
    }
    xs = {name: s[0] for name, s in streams.items()}
    pend = {name: None for name in streams}
    new_k, new_v = [], []
    for i in range(depth):
        kind, slot = i % N_MIXERS, i // N_MIXERS
        m_i = mods[i]
        gf = norm_ffn_g[i]
        last = i == depth - 1
        for name, (_, n, mod_row) in streams.items():
            x, pending = xs[name], pend[name]
            if kind == 0:
                x, h2 = _pool_mixer(x, pending, m_i, mod_row, norm_mix_g[i], gf, cast(pool_w[slot]),
                                    pool_scale[slot], n)
            elif kind == 1:
                x, h2 = _sgu_mixer(x, pending, m_i, mod_row, norm_mix_g[i], gf, cast(sgu_w_in[slot]),
                                   sgu_norm_g[slot], cast(sgu_ws[slot]), sgu_b[slot][:, :, None],
                                   cast(sgu_w_out[slot]))
            else:
                lam_init = 0.8 - 0.6 * math.exp(-0.3 * i)
                is_ctx = name == "ctx"
                outs = _qkv(x, pending, m_i, mod_row, norm_mix_g[i], cast(attn_w_qkv[slot]), n,
                            rope=not is_ctx, keep_f32=is_ctx)
                nb = x.shape[0] // n
                q, k, v = (a.reshape(nb, n, d) for a in outs[:3])
                if is_ctx:
                    new_k.append(outs[3])
                    new_v.append(outs[4])
                a = _attention(q, k, v, None if is_ctx else (cache_k, cache_v, slot),
                               attn_lambda[slot], attn_subln_g[slot], lam_init)
                x, h2 = _attn_out_proj(x, pending, a.reshape(-1, d), cast(attn_w_o[slot]), m_i, mod_row, gf)
            if last:
                xs[name] = _ffn(h2, *ffn_w[i], residual=(x, m_i, mod_row), final_g=final_g)
                pend[name] = None
            elif side_cast and i == 0 and name == "lat":
                f, w_in_rest, w_out_rest = _ffn(h2, *ffn_w[0], casts=(ffn_w_in, ffn_w_out))
                ffn_w += [(w_in_rest.reshape((depth - 1,) + ffn_w_in.shape[1:]),
                           w_out_rest.reshape((depth - 1,) + ffn_w_out.shape[1:]), l - 1)
                          for l in range(1, depth)]
                xs[name], pend[name] = x, (f, m_i)
            else:
                xs[name], pend[name] = x, (_ffn(h2, *ffn_w[i]), m_i)
    heads = d // V_DIM
    y_prompt = xs["ctx"].reshape(batch, seq, d)
    y_sample = xs["lat"].reshape(dec_batch, dec_seq, d)
    new_cache_k = jnp.stack([a.reshape(batch, seq, heads, 2, HEAD_DIM) for a in new_k], axis=1)
    new_cache_v = jnp.stack([a.reshape(batch, seq, heads, V_DIM) for a in new_v], axis=1)
    return (y_prompt, y_sample, new_cache_k, new_cache_v)
```

```python
import functools
import math

import jax
import jax.numpy as jnp
from jax import lax
from jax.experimental import pallas as pl
from jax.experimental.pallas import tpu as pltpu

EPS = 1e-6
N_MOD = 6
N_MIXERS = 3
POOL_WINDOWS = (2, 4, 8, 16)
POOL_HALO = 8
SGU_GROUPS = 8
SGU_CHUNK = 128
HEAD_DIM = 128
V_DIM = 2 * HEAD_DIM
GRID_W = 64
ROPE_BASE = 10000.0
ROPE_HALF = HEAD_DIM // 4
LOG2E = 1.4426950408889634
MXU_DTYPE = jnp.bfloat16
LANES = 128
PACKED_ROWS = 16
COND_ROWS = PACKED_ROWS
V7X_VMEM_BYTES = 64 * 1024 * 1024
VMEM_CAP_BYTES = V7X_VMEM_BYTES - 8 * 1024 * 1024
VMEM_FLOOR_BYTES = 16 * 1024 * 1024

ROW_TILE = 512
FFN_ROW_TILE = 1024
FFN_HIDDEN_TILE = 512
RESIDENT_WEIGHT_ROW_TILE = 256
ADA_COL_TILE = 1024
KV_CHUNK = 512
Q_TILE = 1024


def _vmem_limit(estimate_bytes):
    return int(min(max(estimate_bytes * 5 // 4, VMEM_FLOOR_BYTES), VMEM_CAP_BYTES))


def _params(semantics, vmem_estimate):
    return pltpu.CompilerParams(dimension_semantics=semantics,
                                vmem_limit_bytes=_vmem_limit(vmem_estimate))


def _dot(a, b):
    return jnp.dot(a, b, preferred_element_type=jnp.float32)


def _norm_mod(x, g, shift, scale):
    ms = jnp.mean(x * x, axis=-1, keepdims=True)
    return (x * lax.rsqrt(ms + EPS)) * (g * (1.0 + scale)) + shift


def _emit_ffn_input(h2_ref, x_new, gf_ref, mod_ref):
    h2_ref[...] = _norm_mod(x_new, gf_ref[...], mod_ref[3:4, :], mod_ref[4:5, :]).astype(h2_ref.dtype)


def _split_pending(refs, pending, n=1):
    xs, rest = refs[:n], refs[n:]
    if not pending:
        return xs, None, None, rest
    return xs, rest[:n], rest[n], rest[n + 1:]


def _stream_rows(x_ref, f_ref, modp_ref):
    x = x_ref[...]
    if f_ref is not None:
        x = x + modp_ref[5:6, :] * f_ref[...]
    return x


def _largest_tile(n, cap, quantum):
    t = min(cap, n)
    t -= t % quantum
    while n % t:
        t -= quantum
    return t


def _ada_kernel(cond_ref, w_ref, b_ref, o_ref):
    c = cond_ref[...]
    s = (c * jax.nn.sigmoid(c)).astype(MXU_DTYPE)
    o_ref[...] = _dot(s, w_ref[...].astype(MXU_DTYPE)) + b_ref[...]


def _ada_mods(cond, ada_w, ada_b):
    depth, d, n = ada_w.shape
    bn = _largest_tile(n, ADA_COL_TILE, LANES)
    est = 2 * d * bn * 4 + d * bn * 2 + 4 * COND_ROWS * (d + bn) * 4
    return pl.pallas_call(
        _ada_kernel,
        grid=(depth, n // bn),
        in_specs=[
            pl.BlockSpec((COND_ROWS, d), lambda l, j: (0, 0)),
            pl.BlockSpec((None, d, bn), lambda l, j: (l, 0, j)),
            pl.BlockSpec((None, 1, bn), lambda l, j: (l, 0, j)),
        ],
        out_specs=pl.BlockSpec((None, COND_ROWS, bn), lambda l, j: (l, 0, j)),
        out_shape=jax.ShapeDtypeStruct((depth, COND_ROWS, n), jnp.float32),
        compiler_params=_params(("parallel", "parallel"), est),
        name="ada_mods",
    )(cond, ada_w, ada_b.reshape(depth, 1, n))


def _ffn_kernel(*refs, n_hidden_blocks, residual, final_norm, n_casts):
    refs = list(refs)
    if residual:
        x_ref, mod_ref = refs[:2]
        refs = refs[2:]
    h_ref, wa_ref, wb_ref, wo_ref = refs[:4]
    refs = refs[4:]
    if final_norm:
        fg_ref = refs.pop(0)
    cast_src, refs = refs[:n_casts], refs[n_casts:]
    o_ref, cast_dst = refs[0], refs[1:]
    j = pl.program_id(1)

    @pl.when(j == 0)
    def _():
        o_ref[...] = jnp.zeros_like(o_ref)

    for src, dst in zip(cast_src, cast_dst):
        dst[...] = src[...].astype(dst.dtype)

    h = h_ref[...]
    a = _dot(h, wa_ref[...])
    b = _dot(h, wb_ref[...])
    act = (a * jax.nn.sigmoid(a)) * b
    o_ref[...] += _dot(act.astype(MXU_DTYPE), wo_ref[...])

    if residual:
        @pl.when(j == n_hidden_blocks - 1)
        def _():
            o_ref[...] = x_ref[...] + mod_ref[5:6, :] * o_ref[...]
            if final_norm:
                y = o_ref[...]
                ms = jnp.mean(y * y, axis=-1, keepdims=True)
                o_ref[...] = (o_ref[...] * lax.rsqrt(ms + EPS)) * fg_ref[...]


def _cast_plan(shape, ni, nj):
    layers, r, c = shape
    rows = layers * r
    if rows % ni == 0 and (rows // ni) % PACKED_ROWS == 0 and c % nj == 0 and (c // nj) % LANES == 0:
        return pl.BlockSpec((rows // ni, c // nj), lambda i, j: (i, j))
    if rows % (ni * nj) == 0 and (rows // (ni * nj)) % PACKED_ROWS == 0:
        return pl.BlockSpec((rows // (ni * nj), c), lambda i, j: (i * nj + j, 0))
    return None


def _ffn_grid(t, f, fused_residual=False):
    tm = _largest_tile(t, ROW_TILE if fused_residual else FFN_ROW_TILE, PACKED_ROWS)
    th = _largest_tile(f, FFN_HIDDEN_TILE, LANES)
    return tm, th, t // tm, f // th


def _ffn(h, w_in, w_out, layer, residual=None, final_g=None, casts=()):
    t, d = h.shape
    f = w_out.shape[1]
    fused = residual is not None
    tm, th, ni, nj = _ffn_grid(t, f, fused)
    row_spec = pl.BlockSpec((tm, d), lambda i, j: (i, 0))
    in_specs, args = [], []
    if fused:
        x, mods, mod_row = residual
        in_specs += [row_spec, pl.BlockSpec((None, N_MOD, d), lambda i, j: (mod_row(i, tm), 0, 0))]
        args += [x, mods]
    in_specs += [
        row_spec,
        pl.BlockSpec((None, d, th), lambda i, j: (layer, 0, j)),
        pl.BlockSpec((None, d, th), lambda i, j: (layer, 0, j + nj)),
        pl.BlockSpec((None, th, d), lambda i, j: (layer, j, 0)),
    ]
    args += [h, w_in, w_in, w_out]
    if final_g is not None:
        in_specs.append(pl.BlockSpec((1, d), lambda i, j: (0, 0)))
        args.append(final_g.reshape(1, d))
    cast_specs = [_cast_plan(a.shape, ni, nj) for a in casts]
    in_specs += cast_specs
    args += [a.reshape(-1, a.shape[-1]) for a in casts]
    est = (4 if fused else 2) * tm * d * 4 + 2 * tm * d * 2 + 6 * d * th * 2 + 7 * tm * th * 4 \
        + sum(2 * 6 * math.prod(s.block_shape) for s in cast_specs)
    out = pl.pallas_call(
        functools.partial(_ffn_kernel, n_hidden_blocks=nj, residual=fused, final_norm=final_g is not None,
                          n_casts=len(casts)),
        grid=(ni, nj),
        in_specs=in_specs,
        out_specs=[row_spec] + cast_specs,
        out_shape=[jax.ShapeDtypeStruct((t, d), jnp.float32)]
        + [jax.ShapeDtypeStruct((a.shape[0] * a.shape[1], a.shape[2]), MXU_DTYPE) for a in casts],
        compiler_params=_params(("parallel", "arbitrary"), est),
        name="ffn",
    )(*args)
    if not casts:
        return out[0]
    return [out[0]] + [o.reshape(a.shape) for o, a in zip(out[1:], casts)]


def _pool_kernel(*refs, seq_len, blocks_per_seq, pending):
    (x_ref, xp_ref, xn_ref), fs, modp_ref, rest = _split_pending(list(refs), pending, 3)
    f_ref, fp_ref, fn_ref = fs if pending else (None, None, None)
    mod_ref, g_ref, gf_ref, w_ref, sc_ref, o_ref, h2_ref, h_sc, p_sc, d_sc = rest
    tm, d = x_ref.shape
    gd = d // len(POOL_WINDOWS)
    n = tm + 2 * POOL_HALO
    i = pl.program_id(0)
    blk = i % blocks_per_seq
    g = g_ref[...]
    shift = mod_ref[0:1, :]
    scale = mod_ref[1:2, :]
    x = _stream_rows(x_ref, f_ref, modp_ref)
    has_prev = (blk > 0).astype(jnp.float32)
    has_next = (blk < blocks_per_seq - 1).astype(jnp.float32)
    h_sc[0:POOL_HALO, :] = _norm_mod(_stream_rows(xp_ref, fp_ref, modp_ref), g, shift, scale) * has_prev
    h_sc[POOL_HALO:POOL_HALO + tm, :] = _norm_mod(x, g, shift, scale)
    h_sc[POOL_HALO + tm:n, :] = _norm_mod(_stream_rows(xn_ref, fn_ref, modp_ref), g, shift, scale) * has_next
    h_sc[n:, :] = jnp.zeros((POOL_HALO, d), jnp.float32)
    p_sc[n:, :] = jnp.zeros((POOL_HALO, gd), jnp.float32)

    pos = blk * tm + lax.broadcasted_iota(jnp.int32, (tm, 1), 0)
    for gi, win in enumerate(POOL_WINDOWS):
        cols = slice(gi * gd, (gi + 1) * gd)
        half = win // 2
        if half == 1:
            lo = h_sc[POOL_HALO - 1:POOL_HALO - 1 + tm, cols]
            hi = h_sc[POOL_HALO:POOL_HALO + tm, cols]
        else:
            p_sc[0:n, :] = h_sc[0:n, cols] + h_sc[1:n + 1, cols]
            s = 2
            while s < half:
                p_sc[0:n, :] = p_sc[0:n, :] + p_sc[s:n + s, :]
                s *= 2
            lo = p_sc[POOL_HALO - half:POOL_HALO - half + tm, :]
            hi = p_sc[POOL_HALO:POOL_HALO + tm, :]
        cnt = jnp.minimum(pos + half, seq_len) - jnp.maximum(pos - half, 0)
        inv_cnt = 1.0 / cnt.astype(jnp.float32)
        dev = (lo + hi) * inv_cnt - h_sc[POOL_HALO:POOL_HALO + tm, cols]
        d_sc[:, cols] = _dot(dev.astype(MXU_DTYPE), w_ref[gi])
    x_new = x + d_sc[...] * (mod_ref[2:3, :] * sc_ref[...])
    o_ref[...] = x_new
    _emit_ffn_input(h2_ref, x_new, gf_ref, mod_ref)


def _pool_mixer(x, pending, mods, mod_row, g, gf, w, scale, seq_len):
    t, d = x.shape
    tm = _largest_tile(seq_len, ROW_TILE, PACKED_ROWS)
    bps = seq_len // tm
    hb = tm // POOL_HALO
    n_halo_blocks = t // POOL_HALO
    gd = d // len(POOL_WINDOWS)
    row_spec = pl.BlockSpec((tm, d), lambda i: (i, 0))
    vec_spec = pl.BlockSpec((1, d), lambda i: (0, 0))
    mod_spec = pl.BlockSpec((None, N_MOD, d), lambda i: (mod_row(i, tm), 0, 0))
    halo_specs = [
        row_spec,
        pl.BlockSpec((POOL_HALO, d), lambda i: (jnp.maximum(i * hb - 1, 0), 0)),
        pl.BlockSpec((POOL_HALO, d), lambda i: (jnp.minimum((i + 1) * hb, n_halo_blocks - 1), 0)),
    ]
    in_specs, args = list(halo_specs), [x, x, x]
    if pending:
        f, mods_prev = pending
        in_specs += halo_specs + [mod_spec]
        args += [f, f, f, mods_prev]
    in_specs += [mod_spec, vec_spec, vec_spec, pl.BlockSpec(w.shape, lambda i: (0, 0, 0)), vec_spec]
    args += [mods, g.reshape(1, d), gf.reshape(1, d), w, scale.reshape(1, d)]
    est = (6 if pending else 4) * tm * d * 4 + 2 * tm * d * 2 + 2 * (tm + 3 * POOL_HALO) * d * 4 \
        + 2 * w.size * 2 + 6 * tm * gd * 4
    return pl.pallas_call(
        functools.partial(_pool_kernel, seq_len=seq_len, blocks_per_seq=bps, pending=bool(pending)),
        grid=(t // tm,),
        in_specs=in_specs,
        out_specs=[row_spec, row_spec],
        out_shape=[jax.ShapeDtypeStruct((t, d), jnp.float32), jax.ShapeDtypeStruct((t, d), MXU_DTYPE)],
        scratch_shapes=[pltpu.VMEM((tm + 3 * POOL_HALO, d), jnp.float32),
                        pltpu.VMEM((tm + 3 * POOL_HALO, gd), jnp.float32),
                        pltpu.VMEM((tm, d), jnp.float32)],
        compiler_params=_params(("parallel",), est),
        name="pool_mixer",
    )(*args)


def _sgu_kernel(*refs, pending):
    (x_ref,), fs, modp_ref, rest = _split_pending(list(refs), pending)
    mod_ref, g_ref, gf_ref, win_ref, ng_ref, ws_ref, bs_ref, wout_ref, o_ref, h2_ref, h_sc, v_sc, p_sc = rest
    tm, d = x_ref.shape
    sd = wout_ref.shape[0]
    gd = sd // SGU_GROUPS
    x = _stream_rows(x_ref, fs[0] if pending else None, modp_ref)
    h_sc[...] = _norm_mod(x, g_ref[...], mod_ref[0:1, :], mod_ref[1:2, :]).astype(MXU_DTYPE)
    h = h_sc[...]

    def gelu(z):
        return 0.5 * z * (1.0 + lax.erf(z * (1.0 / math.sqrt(2.0))))

    ssq = jnp.zeros((tm, 1), jnp.float32)
    for gi in range(SGU_GROUPS):
        v = gelu(_dot(h, win_ref[:, sd + gi * gd:sd + (gi + 1) * gd]))
        ssq = ssq + jnp.sum(v * v, axis=-1, keepdims=True)
        v_sc[:, gi * gd:(gi + 1) * gd] = v
    rstd = lax.rsqrt(ssq * (1.0 / sd) + EPS)

    for gi in range(SGU_GROUPS):
        cols = slice(gi * gd, (gi + 1) * gd)
        vn = (v_sc[:, cols] * rstd * ng_ref[:, cols]).astype(MXU_DTYPE)
        u = gelu(_dot(h, win_ref[:, cols]))
        wsg = ws_ref[gi]
        bias = bs_ref[gi]
        for c in range(tm // SGU_CHUNK):
            rows = slice(c * SGU_CHUNK, (c + 1) * SGU_CHUNK)
            vm = _dot(wsg, vn[rows, :]) + bias
            p_sc[rows, cols] = (u[rows, :] * vm).astype(MXU_DTYPE)
    x_new = x + mod_ref[2:3, :] * _dot(p_sc[...], wout_ref[...])
    o_ref[...] = x_new
    _emit_ffn_input(h2_ref, x_new, gf_ref, mod_ref)


def _sgu_mixer(x, pending, mods, mod_row, g, gf, w_in, norm_g, ws, bs, w_out):
    t, d = x.shape
    sd = w_out.shape[0]
    tm = _largest_tile(t, RESIDENT_WEIGHT_ROW_TILE, SGU_CHUNK)
    est = (w_in.size + w_out.size + ws.size) * 2 + (6 if pending else 4) * tm * d * 4 + tm * sd * 4 \
        + 4 * tm * d * 2 + 6 * tm * (sd // SGU_GROUPS) * 4
    const = dict(pipeline_mode=pl.Buffered(1))
    row_spec = pl.BlockSpec((tm, d), lambda i: (i, 0))
    vec_spec = pl.BlockSpec((1, d), lambda i: (0, 0))
    mod_spec = pl.BlockSpec((None, N_MOD, d), lambda i: (mod_row(i, tm), 0, 0))
    in_specs, args = [row_spec], [x]
    if pending:
        in_specs += [row_spec, mod_spec]
        args += list(pending)
    in_specs += [
        mod_spec,
        vec_spec,
        vec_spec,
        pl.BlockSpec(w_in.shape, lambda i: (0, 0), **const),
        pl.BlockSpec((1, sd), lambda i: (0, 0)),
        pl.BlockSpec(ws.shape, lambda i: (0, 0, 0), **const),
        pl.BlockSpec(bs.shape, lambda i: (0, 0, 0), **const),
        pl.BlockSpec(w_out.shape, lambda i: (0, 0), **const),
    ]
    args += [mods, g.reshape(1, d), gf.reshape(1, d), w_in, norm_g.reshape(1, sd), ws, bs, w_out]
    return pl.pallas_call(
        functools.partial(_sgu_kernel, pending=bool(pending)),
        grid=(t // tm,),
        in_specs=in_specs,
        out_specs=[row_spec, row_spec],
        out_shape=[jax.ShapeDtypeStruct((t, d), jnp.float32), jax.ShapeDtypeStruct((t, d), MXU_DTYPE)],
        scratch_shapes=[pltpu.VMEM((tm, d), MXU_DTYPE), pltpu.VMEM((tm, sd), jnp.float32),
                        pltpu.VMEM((tm, sd), MXU_DTYPE)],
        compiler_params=_params(("parallel",), est),
        name="sgu_mixer",
    )(*args)


def _rope_tables(n_tokens):
    rows = n_tokens // GRID_W
    row = jnp.broadcast_to(jnp.arange(rows, dtype=jnp.float32)[:, None], (rows, GRID_W)).reshape(-1)
    col = jnp.broadcast_to(jnp.arange(GRID_W, dtype=jnp.float32)[None, :], (rows, GRID_W)).reshape(-1)
    inv = ROPE_BASE ** (-jnp.arange(ROPE_HALF, dtype=jnp.float32) / ROPE_HALF)
    ang = jnp.stack([row[:, None] * inv, col[:, None] * inv], axis=1)
    cos, sin = jnp.cos(ang), jnp.sin(ang)
    c = jnp.stack([cos, cos], axis=2).reshape(n_tokens, HEAD_DIM)
    s = jnp.stack([-sin, sin], axis=2).reshape(n_tokens, HEAD_DIM)
    return c, s


def _qkv_kernel(*refs, pending, rope, keep_f32, q_scale):
    (x_ref,), fs, modp_ref, rest = _split_pending(list(refs), pending)
    mod_ref, g_ref, w_ref = rest[:3]
    rest = rest[3:]
    if rope:
        c_ref, s_ref = rest[:2]
        rest = rest[2:]
    q_ref, k_ref, v_ref = rest[:3]
    rest = rest[3:]
    if keep_f32:
        kf_ref, vf_ref = rest
    tm, d = x_ref.shape

    def rotate(dst_ref, y, mult):
        c = c_ref[...]
        s = s_ref[...]
        lane = lax.broadcasted_iota(jnp.int32, (tm, HEAD_DIM), 1)
        low_half = (lane % (2 * ROPE_HALF)) < ROPE_HALF
        for gi in range(d // HEAD_DIM):
            cols = slice(gi * HEAD_DIM, (gi + 1) * HEAD_DIM)
            yg = y[:, cols]
            partner = jnp.where(low_half,
                                pltpu.roll(yg, HEAD_DIM - ROPE_HALF, 1),
                                pltpu.roll(yg, ROPE_HALF, 1))
            r = yg * c + partner * s
            if mult != 1.0:
                r = r * mult
            dst_ref[:, cols] = r.astype(dst_ref.dtype)

    x = _stream_rows(x_ref, fs[0] if pending else None, modp_ref)
    h = _norm_mod(x, g_ref[...], mod_ref[0:1, :], mod_ref[1:2, :]).astype(MXU_DTYPE)

    y = _dot(h, w_ref[:, 0:d])
    if rope:
        rotate(q_ref, y, q_scale)
    else:
        q_ref[...] = (y * q_scale).astype(q_ref.dtype)

    y = _dot(h, w_ref[:, d:2 * d])
    if keep_f32:
        kf_ref[...] = y
    if rope:
        rotate(k_ref, y, 1.0)
    else:
        k_ref[...] = y.astype(k_ref.dtype)

    y = _dot(h, w_ref[:, 2 * d:3 * d])
    if keep_f32:
        vf_ref[...] = y
    v_ref[...] = y.astype(v_ref.dtype)


def _qkv(x, pending, mods, mod_row, g, w_qkv, seq_len, rope, keep_f32):
    t, d = x.shape
    tm = _largest_tile(seq_len, RESIDENT_WEIGHT_ROW_TILE, PACKED_ROWS)
    bps = seq_len // tm
    q_scale = HEAD_DIM ** -0.5 * LOG2E
    row_spec = pl.BlockSpec((tm, d), lambda i: (i, 0))
    mod_spec = pl.BlockSpec((None, N_MOD, d), lambda i: (mod_row(i, tm), 0, 0))
    in_specs, args = [row_spec], [x]
    if pending:
        in_specs += [row_spec, mod_spec]
        args += list(pending)
    in_specs += [mod_spec, pl.BlockSpec((1, d), lambda i: (0, 0)),
                 pl.BlockSpec(w_qkv.shape, lambda i: (0, 0), pipeline_mode=pl.Buffered(1))]
    args += [mods, g.reshape(1, d), w_qkv]
    if rope:
        c, s = _rope_tables(seq_len)
        in_specs += [pl.BlockSpec((tm, HEAD_DIM), lambda i: (i % bps, 0))] * 2
        args += [c, s]
    out_specs = [row_spec] * 3
    out_shape = [jax.ShapeDtypeStruct((t, d), MXU_DTYPE)] * 3
    if keep_f32:
        out_specs += [row_spec] * 2
        out_shape += [jax.ShapeDtypeStruct((t, d), jnp.float32)] * 2
    est = w_qkv.size * 2 + (4 if pending else 2) * tm * d * 4 + 7 * tm * d * 2 + 4 * tm * d * 4 \
        + (4 * tm * d * 4 if keep_f32 else 0)
    return pl.pallas_call(
        functools.partial(_qkv_kernel, pending=bool(pending), rope=rope, keep_f32=keep_f32, q_scale=q_scale),
        grid=(t // tm,),
        in_specs=in_specs,
        out_specs=out_specs,
        out_shape=out_shape,
        compiler_params=_params(("parallel",), est),
        name="attn_qkv",
    )(*args)


def _attn_kernel(q_ref, k_ref, v_ref, *rest, lam_init, has_cache):
    if has_cache:
        ck_ref, cv_ref, lp_ref, sg_ref, o_ref = rest
        cached_v = cv_ref[...].astype(MXU_DTYPE)
    else:
        lp_ref, sg_ref, o_ref = rest
    n_new = k_ref.shape[0]
    chunk = min(KV_CHUNK, n_new)
    lp = lp_ref[...]
    lam = (jnp.exp(jnp.sum(lp[0:1, :] * lp[1:2, :], axis=-1, keepdims=True))
           - jnp.exp(jnp.sum(lp[2:3, :] * lp[3:4, :], axis=-1, keepdims=True)) + lam_init)
    for hd in range(q_ref.shape[1] // V_DIM):
        vcols = slice(hd * V_DIM, (hd + 1) * V_DIM)
        outs = []
        for mp in range(2):
            cols = slice(hd * V_DIM + mp * HEAD_DIM, hd * V_DIM + (mp + 1) * HEAD_DIM)
            q = q_ref[:, cols]
            kv = []
            if has_cache:
                kv.append((ck_ref[:, cols].astype(MXU_DTYPE), cached_v[:, vcols]))
            for c0 in range(0, n_new, chunk):
                kv.append((k_ref[c0:c0 + chunk, cols], v_ref[c0:c0 + chunk, vcols]))
            m = denom = acc = None
            for kc, vc in kv:
                s = lax.dot_general(q, kc, (((1,), (1,)), ((), ())), preferred_element_type=jnp.float32)
                mc = jnp.max(s, axis=-1, keepdims=True)
                if m is None:
                    m = mc
                    p = jnp.exp2(s - m)
                    denom = jnp.sum(p, axis=-1, keepdims=True)
                    acc = _dot(p.astype(MXU_DTYPE), vc)
                else:
                    m_new = jnp.maximum(m, mc)
                    alpha = jnp.exp2(m - m_new)
                    p = jnp.exp2(s - m_new)
                    denom = alpha * denom + jnp.sum(p, axis=-1, keepdims=True)
                    acc = alpha * acc + _dot(p.astype(MXU_DTYPE), vc)
                    m = m_new
            outs.append(acc / denom)
        o = outs[0] - lam * outs[1]
        ms = jnp.mean(o * o, axis=-1, keepdims=True)
        o = (o * lax.rsqrt(ms + EPS) * sg_ref[...]) * (1.0 - lam_init)
        o_ref[:, vcols] = o.astype(o_ref.dtype)


def _attention(q, k, v, cache, lam_params, subln_g, lam_init):
    b, n, d = q.shape
    tq = _largest_tile(n, Q_TILE, PACKED_ROWS)
    past = 0 if cache is None else cache[0].shape[2]
    hw = d if n + past <= KV_CHUNK else V_DIM
    in_specs = [
        pl.BlockSpec((None, tq, hw), lambda bi, hi, qi: (bi, qi, hi)),
        pl.BlockSpec((None, n, hw), lambda bi, hi, qi: (bi, 0, hi)),
        pl.BlockSpec((None, n, hw), lambda bi, hi, qi: (bi, 0, hi)),
    ]
    args = [q, k, v]
    if cache is not None:
        cache_k, cache_v, slot = cache
        in_specs += [pl.BlockSpec((None, None, past, hw), lambda bi, hi, qi: (bi, slot, 0, hi))] * 2
        args += [cache_k, cache_v]
    in_specs += [
        pl.BlockSpec(lam_params.shape, lambda bi, hi, qi: (0, 0)),
        pl.BlockSpec((1, V_DIM), lambda bi, hi, qi: (0, 0)),
    ]
    args += [lam_params, subln_g.reshape(1, V_DIM)]
    chunk = min(KV_CHUNK, n)
    est = 4 * tq * hw * 2 + 4 * n * hw * 2 + 4 * past * hw * 4 \
        + (hw // V_DIM) * (6 * tq * chunk * 4 + 8 * tq * V_DIM * 4)
    return pl.pallas_call(
        functools.partial(_attn_kernel, lam_init=lam_init, has_cache=cache is not None),
        grid=(b, d // hw, n // tq),
        in_specs=in_specs,
        out_specs=pl.BlockSpec((None, tq, hw), lambda bi, hi, qi: (bi, qi, hi)),
        out_shape=jax.ShapeDtypeStruct((b, n, d), MXU_DTYPE),
        compiler_params=_params(("parallel", "parallel", "parallel"), est),
        name="diff_attention",
    )(*args)


def _proj_kernel(*refs, pending):
    (x_ref,), fs, modp_ref, rest = _split_pending(list(refs), pending)
    a_ref, w_ref, mod_ref, gf_ref, o_ref, h2_ref = rest
    half = x_ref.shape[0] // 2
    for r0 in (0, half):
        rows = slice(r0, r0 + half)
        x = _stream_rows(x_ref.at[rows], fs[0].at[rows] if pending else None, modp_ref)
        x_new = x + mod_ref[2:3, :] * _dot(a_ref[rows, :], w_ref[...])
        o_ref[rows, :] = x_new
        _emit_ffn_input(h2_ref.at[rows], x_new, gf_ref, mod_ref)


def _attn_out_proj(x, pending, a, w_o, mods, mod_row, gf):
    t, d = x.shape
    tm = _largest_tile(t, ROW_TILE, PACKED_ROWS)
    est = 4 * tm * d * 2 + 2 * d * d * 2 + (8 if pending else 6) * tm * d * 4
    row_spec = pl.BlockSpec((tm, d), lambda i: (i, 0))
    mod_spec = pl.BlockSpec((None, N_MOD, d), lambda i: (mod_row(i, tm), 0, 0))
    in_specs, args = [row_spec], [x]
    if pending:
        in_specs += [row_spec, mod_spec]
        args += list(pending)
    in_specs += [row_spec, pl.BlockSpec((d, d), lambda i: (0, 0)), mod_spec, pl.BlockSpec((1, d), lambda i: (0, 0))]
    args += [a, w_o, mods, gf.reshape(1, d)]
    return pl.pallas_call(
        functools.partial(_proj_kernel, pending=bool(pending)),
        grid=(t // tm,),
        in_specs=in_specs,
        out_specs=[row_spec, row_spec],
        out_shape=[jax.ShapeDtypeStruct((t, d), jnp.float32), jax.ShapeDtypeStruct((t, d), MXU_DTYPE)],
        compiler_params=_params(("parallel",), est),
        name="attn_out_proj",
    )(*args)


def kernel(x_prompt, x_sample, cache_k, cache_v, c, c_ctx, ada_w, ada_b, norm_mix_g, norm_ffn_g,
           pool_w, pool_scale, sgu_w_in, sgu_norm_g, sgu_ws, sgu_b, sgu_w_out,
           attn_w_qkv, attn_lambda, attn_subln_g, attn_w_o, ffn_w_in, ffn_w_out, final_g):
    batch, seq, d = x_prompt.shape
    dec_batch, dec_seq, _ = x_sample.shape
    depth = ada_w.shape[0]
    assert dec_batch < COND_ROWS and d % V_DIM == 0 and dec_seq % GRID_W == 0

    cond = jnp.zeros((COND_ROWS, d), jnp.float32).at[:dec_batch].set(c).at[dec_batch].set(c_ctx)
    mods = _ada_mods(cond, ada_w, ada_b).reshape(depth, COND_ROWS, N_MOD, d)

    cast = lambda w: w.astype(MXU_DTYPE)
    hidden = ffn_w_out.shape[1]
    _, _, ni, nj = _ffn_grid(dec_batch * dec_seq, hidden)
    side_cast = depth > 1 and all(_cast_plan(w.shape, ni, nj) for w in (ffn_w_in, ffn_w_out))
    if side_cast:
        ffn_w = [(cast(ffn_w_in[:1]), cast(ffn_w_out[:1]), 0)]
    else:
        w_in_all, w_out_all = cast(ffn_w_in), cast(ffn_w_out)
        ffn_w = [(w_in_all, w_out_all, l) for l in range(depth)]
    cache_k = cache_k.reshape(cache_k.shape[:3] + (d,))
    cache_v = cache_v.reshape(cache_v.shape[:3] + (d,))
    streams = {
        "ctx": (x_prompt.reshape(batch * seq, d), seq, lambda i, tm: dec_batch),
        "lat": (x_sample.reshape(dec_batch * dec_seq, d), dec_seq,
                lambda i, tm: (i * tm) // dec_seq),
    }
    xs = {name: s[0] for name, s in streams.items()}
    pend = {name: None for name in streams}
    new_k, new_v = [], []
    for i in range(depth):
        kind, slot = i % N_MIXERS, i // N_MIXERS
        m_i = mods[i]
        gf = norm_ffn_g[i]
        last = i == depth - 1
        for name, (_, n, mod_row) in streams.items():
            x, pending = xs[name], pend[name]
            if kind == 0:
                x, h2 = _pool_mixer(x, pending, m_i, mod_row, norm_mix_g[i], gf, cast(pool_w[slot]),
                                    pool_scale[slot], n)
            elif kind == 1:
                x, h2 = _sgu_mixer(x, pending, m_i, mod_row, norm_mix_g[i], gf, cast(sgu_w_in[slot]),
                                   sgu_norm_g[slot], cast(sgu_ws[slot]), sgu_b[slot][:, :, None],
                                   cast(sgu_w_out[slot]))
            else:
                lam_init = 0.8 - 0.6 * math.exp(-0.3 * i)
                is_ctx = name == "ctx"
                outs = _qkv(x, pending, m_i, mod_row, norm_mix_g[i], cast(attn_w_qkv[slot]), n,
                            rope=not is_ctx, keep_f32=is_ctx)
                nb = x.shape[0] // n
                q, k, v = (a.reshape(nb, n, d) for a in outs[:3])
                if is_ctx:
                    new_k.append(outs[3])
                    new_v.append(outs[4])
                a = _attention(q, k, v, None if is_ctx else (cache_k, cache_v, slot),
                               attn_lambda[slot], attn_subln_g[slot], lam_init)
                x, h2 = _attn_out_proj(x, pending, a.reshape(-1, d), cast(attn_w_o[slot]), m_i, mod_row, gf)
            if last:
                xs[name] = _ffn(h2, *ffn_w[i], residual=(x, m_i, mod_row), final_g=final_g)
                pend[name] = None
            elif side_cast and i == 0 and name == "lat":
                f, w_in_all, w_out_all = _ffn(h2, *ffn_w[0], casts=(ffn_w_in, ffn_w_out))
                ffn_w += [(w_in_all, w_out_all, l) for l in range(1, depth)]
                xs[name], pend[name] = x, (f, m_i)
            else:
                xs[name], pend[name] = x, (_ffn(h2, *ffn_w[i]), m_i)
    heads = d // V_DIM
    y_prompt = xs["ctx"].reshape(batch, seq, d)
    y_sample = xs["lat"].reshape(dec_batch, dec_seq, d)
    new_cache_k = jnp.stack([a.reshape(batch, seq, heads, 2, HEAD_DIM) for a in new_k], axis=1)
    new_cache_v = jnp.stack([a.reshape(batch, seq, heads, V_DIM) for a in new_v], axis=1)
    return (y_prompt, y_sample, new_cache_k, new_cache_v)
```
